```python
import math
import jax, jax.numpy as jnp
from jax import lax
import numpy as np

D_MODEL = 1024
BATCH = 32
SEQ = 2048
DEPTH = 1
DEC_BATCH = 128
DEC_SEQ = 4
PAST_LEN = 8192
PAGE_SIZE = 128

POOL_WINDOWS = (2, 4, 8, 16)
POOL_GROUPS = 4
POOL_GROUP_WIDTH = 128
POOL_WIDTH = POOL_GROUPS * POOL_GROUP_WIDTH
POOL_STATE = max(POOL_WINDOWS) - 1
ATT_CONFIGS = ((128, 1), (512, 4), (2048, 16))
N_ATT_GROUPS = len(ATT_CONFIGS)
HEADS_PER_GROUP = 8
HEAD_DIM = 64
N_ATT_HEADS = N_ATT_GROUPS * HEADS_PER_GROUP
ATT_WIDTH = N_ATT_HEADS * HEAD_DIM
ATT_OUT_WIDTH = HEADS_PER_GROUP * HEAD_DIM
D_FF = 2816
EPS = 1e-6
IN_WIDTH = POOL_WIDTH + 3 * ATT_WIDTH + 2 * D_MODEL

kernel_name = 'hybrid_pool_dilated_attn_macaron_step'


def rmsnorm(x, g):
    xf = x.astype(jnp.float32)
    y = xf * lax.rsqrt(jnp.mean(xf * xf, axis=-1, keepdims=True) + EPS)
    return (y * g.astype(jnp.float32)).astype(x.dtype)


def swiglu(x, w_gu, w_down):
    a, b = jnp.split(x @ w_gu, 2, axis=-1)
    return (jax.nn.silu(a) * b) @ w_down


def pool_branch(u_ctx, n_ctx, pool_w, pool_scale):
    B, L, C = u_ctx.shape
    rows = np.arange(n_ctx, L)
    uf = u_ctx.astype(jnp.float32)
    cs = jnp.concatenate([jnp.zeros((B, 1, C), jnp.float32), jnp.cumsum(uf, axis=1)], axis=1)
    hi = cs[:, n_ctx + 1:]
    u_new = uf[:, n_ctx:]
    parts = []
    for gi, w in enumerate(POOL_WINDOWS):
        c0, c1 = gi * POOL_GROUP_WIDTH, (gi + 1) * POOL_GROUP_WIDTH
        lo = cs[:, np.maximum(rows + 1 - w, 0), c0:c1]
        cnt = jnp.asarray(np.minimum(w, rows + 1), jnp.float32)[None, :, None]
        parts.append((hi[..., c0:c1] - lo) / cnt - u_new[..., c0:c1])
    pooled = jnp.stack(parts, axis=2).astype(u_ctx.dtype)
    mixed = jnp.einsum('bngc,gcd->bngd', pooled, pool_w)
    return mixed.reshape(B, L - n_ctx, POOL_WIDTH) * pool_scale


def dilated_attn_prompt(q, k, v, window, dil):
    B, S, H, E = q.shape
    blk = window // dil
    L = -(-S // dil)
    Lp = -(-L // blk) * blk
    Sp = Lp * dil
    nb = Lp // blk
    pad = ((0, 0), (0, Sp - S), (0, 0), (0, 0))
    def to_stream(a):
        return jnp.pad(a, pad).reshape(B, Lp, dil, H, E)
    qs, ks, vs = to_stream(q), to_stream(k), to_stream(v)
    qb = qs.reshape(B, nb, blk, dil, H, E)
    def band(a):
        ap = jnp.pad(a, ((0, 0), (blk, 0), (0, 0), (0, 0), (0, 0)))
        prev = ap[:, :Lp].reshape(B, nb, blk, dil, H, E)
        cur = a.reshape(B, nb, blk, dil, H, E)
        return jnp.concatenate([prev, cur], axis=2)
    kb, vb = band(ks), band(vs)
    s = jnp.einsum('bnidhe,bnjdhe->bndhij', qb, kb, preferred_element_type=jnp.float32) * (E ** -0.5)
    i = np.arange(blk)[:, None]
    j = np.arange(2 * blk)[None, :]
    n = np.arange(nb)[:, None, None]
    mask = (j >= i) & (j <= i + blk) & ((n > 0) | (j >= blk))
    s = jnp.where(mask[:, None, None], s, -jnp.inf)
    mx = jnp.max(s, axis=-1, keepdims=True)
    e = jnp.exp(s - mx)
    den = jnp.sum(e, axis=-1, keepdims=True)
    o = jnp.einsum('bndhij,bnjdhe->bnidhe', (e / den).astype(v.dtype), vb, preferred_element_type=jnp.float32)
    lse = (mx + jnp.log(den))[..., 0]
    o = o.reshape(B, Sp, H, E)[:, :S]
    lse = lse.transpose(0, 1, 4, 2, 3).reshape(B, Sp, H)[:, :S]
    return o, lse


def dilated_attn_sample(q, k_all, v_all, window, dil):
    B, T, H, E = q.shape
    n_ctx = k_all.shape[1] - T
    blk = window // dil
    idx = n_ctx + np.arange(T)[:, None] - dil * np.arange(blk + 1)[None, :]
    valid = idx >= 0
    idx_c = np.maximum(idx, 0)
    kg = k_all[:, idx_c]
    vg = v_all[:, idx_c]
    s = jnp.einsum('bthe,btkhe->bthk', q, kg, preferred_element_type=jnp.float32) * (E ** -0.5)
    s = jnp.where(valid[None, :, None, :], s, -jnp.inf)
    mx = jnp.max(s, axis=-1, keepdims=True)
    e = jnp.exp(s - mx)
    den = jnp.sum(e, axis=-1, keepdims=True)
    o = jnp.einsum('bthk,btkhe->bthe', (e / den).astype(v_all.dtype), vg, preferred_element_type=jnp.float32)
    lse = (mx + jnp.log(den))[..., 0]
    return o, lse


def decoder_layer(x, pool_ctx, kv_ctx, ffn1_norm, ffn1_w_gu, ffn1_w_down, mix_norm, w_in,
                  q_norm, k_norm, pool_w, pool_scale, w_branch_pool, w_branch_att, w_out,
                  ffn2_norm, ffn2_w_gu, ffn2_w_down):
    x = x + 0.5 * swiglu(rmsnorm(x, ffn1_norm), ffn1_w_gu, ffn1_w_down)
    h = rmsnorm(x, mix_norm)
    B, S, _ = h.shape
    splits = [POOL_WIDTH, POOL_WIDTH + ATT_WIDTH, POOL_WIDTH + 2 * ATT_WIDTH, POOL_WIDTH + 3 * ATT_WIDTH]
    u, q, k, v, gate = jnp.split(h @ w_in, splits, axis=-1)
    q = rmsnorm(q.reshape(B, S, N_ATT_HEADS, HEAD_DIM), q_norm)
    k = rmsnorm(k.reshape(B, S, N_ATT_HEADS, HEAD_DIM), k_norm)
    v = v.reshape(B, S, N_ATT_HEADS, HEAD_DIM)
    u_ctx = u if pool_ctx is None else jnp.concatenate([pool_ctx.astype(u.dtype), u], axis=1)
    pool_y = pool_branch(u_ctx, u_ctx.shape[1] - S, pool_w, pool_scale)
    new_pool = u_ctx[:, -POOL_STATE:]
    outs, lses, new_kv = [], [], []
    for gi, (window, dil) in enumerate(ATT_CONFIGS):
        hs = slice(gi * HEADS_PER_GROUP, (gi + 1) * HEADS_PER_GROUP)
        qg, kg, vg = q[:, :, hs], k[:, :, hs], v[:, :, hs]
        if kv_ctx is None:
            k_all, v_all = kg, vg
            o, lse = dilated_attn_prompt(qg, kg, vg, window, dil)
        else:
            cache = kv_ctx[gi].astype(kg.dtype)
            k_all = jnp.concatenate([cache[:, :, 0], kg], axis=1)
            v_all = jnp.concatenate([cache[:, :, 1], vg], axis=1)
            o, lse = dilated_attn_sample(qg, k_all, v_all, window, dil)
        keep = min(window, k_all.shape[1])
        new_kv.append(jnp.stack([k_all[:, -keep:], v_all[:, -keep:]], axis=2))
        outs.append(o)
        lses.append(lse)
    wts = jax.nn.softmax(jnp.stack(lses, axis=0), axis=0)
    att = jnp.sum(wts[..., None] * jnp.stack(outs, axis=0), axis=0)
    att = att.reshape(B, S, ATT_OUT_WIDTH).astype(x.dtype)
    g_pool, g_att = jnp.split(jax.nn.sigmoid(gate.astype(jnp.float32)), 2, axis=-1)
    merged = g_pool * (pool_y @ w_branch_pool).astype(jnp.float32) + g_att * (att @ w_branch_att).astype(jnp.float32)
    x = x + merged.astype(x.dtype) @ w_out
    x = x + 0.5 * swiglu(rmsnorm(x, ffn2_norm), ffn2_w_gu, ffn2_w_down)
    return x, new_kv, new_pool


def setup_inputs(seed: int = 0) -> dict:
    key = jax.random.key(seed)
    ks = jax.random.split(key, 24)
    def nrm(k, shape, scale):
        return jax.random.normal(k, shape, jnp.float32) * scale
    def cache_shape(window):
        return (DEPTH, DEC_BATCH, min(window, PAST_LEN), 2, HEADS_PER_GROUP, HEAD_DIM)
    return {
        'x_prompt': nrm(ks[0], (BATCH, SEQ, D_MODEL), 1.0),
        'x_sample': nrm(ks[1], (DEC_BATCH, DEC_SEQ, D_MODEL), 1.0),
        'cache_kv_w128': nrm(ks[2], cache_shape(ATT_CONFIGS[0][0]), 1.0),
        'cache_kv_w512': nrm(ks[3], cache_shape(ATT_CONFIGS[1][0]), 1.0),
        'cache_kv_w2048': nrm(ks[4], cache_shape(ATT_CONFIGS[2][0]), 1.0),
        'state_pool': nrm(ks[5], (DEPTH, DEC_BATCH, POOL_STATE, POOL_WIDTH), 1.0),
        'ffn1_norm': 1.0 + nrm(ks[6], (DEPTH, D_MODEL), 0.02),
        'ffn1_w_gu': nrm(ks[7], (DEPTH, D_MODEL, 2 * D_FF), D_MODEL ** -0.5),
        'ffn1_w_down': nrm(ks[8], (DEPTH, D_FF, D_MODEL), D_FF ** -0.5),
        'mix_norm': 1.0 + nrm(ks[9], (DEPTH, D_MODEL), 0.02),
        'w_in': nrm(ks[10], (DEPTH, D_MODEL, IN_WIDTH), D_MODEL ** -0.5),
        'q_norm': 1.0 + nrm(ks[11], (DEPTH, N_ATT_HEADS, HEAD_DIM), 0.02),
        'k_norm': 1.0 + nrm(ks[12], (DEPTH, N_ATT_HEADS, HEAD_DIM), 0.02),
        'pool_w': nrm(ks[13], (DEPTH, POOL_GROUPS, POOL_GROUP_WIDTH, POOL_GROUP_WIDTH), POOL_GROUP_WIDTH ** -0.5),
        'pool_scale': 1.0 + nrm(ks[14], (DEPTH, POOL_WIDTH), 0.02),
        'w_branch_pool': nrm(ks[15], (DEPTH, POOL_WIDTH, D_MODEL), POOL_WIDTH ** -0.5),
        'w_branch_att': nrm(ks[16], (DEPTH, ATT_OUT_WIDTH, D_MODEL), ATT_OUT_WIDTH ** -0.5),
        'w_out': nrm(ks[17], (DEPTH, D_MODEL, D_MODEL), D_MODEL ** -0.5),
        'ffn2_norm': 1.0 + nrm(ks[18], (DEPTH, D_MODEL), 0.02),
        'ffn2_w_gu': nrm(ks[19], (DEPTH, D_MODEL, 2 * D_FF), D_MODEL ** -0.5),
        'ffn2_w_down': nrm(ks[20], (DEPTH, D_FF, D_MODEL), D_FF ** -0.5),
    }


def reference(x_prompt, x_sample, cache_kv_w128, cache_kv_w512, cache_kv_w2048, state_pool,
              ffn1_norm, ffn1_w_gu, ffn1_w_down, mix_norm, w_in, q_norm, k_norm, pool_w,
              pool_scale, w_branch_pool, w_branch_att, w_out, ffn2_norm, ffn2_w_gu, ffn2_w_down):
    y_prompt, y_sample = x_prompt, x_sample
    kv_p = [[], [], []]
    kv_s = [[], [], []]
    pool_p, pool_s = [], []
    for l in range(DEPTH):
        wl = (ffn1_norm[l], ffn1_w_gu[l], ffn1_w_down[l], mix_norm[l], w_in[l], q_norm[l], k_norm[l],
              pool_w[l], pool_scale[l], w_branch_pool[l], w_branch_att[l], w_out[l],
              ffn2_norm[l], ffn2_w_gu[l], ffn2_w_down[l])
        y_prompt, nkv_p, npool_p = decoder_layer(y_prompt, None, None, *wl)
        y_sample, nkv_s, npool_s = decoder_layer(
            y_sample, state_pool[l], [cache_kv_w128[l], cache_kv_w512[l], cache_kv_w2048[l]], *wl)
        for gi in range(N_ATT_GROUPS):
            kv_p[gi].append(nkv_p[gi])
            kv_s[gi].append(nkv_s[gi])
        pool_p.append(npool_p)
        pool_s.append(npool_s)
    new_kv128_prompt = jnp.stack(kv_p[0])
    new_kv512_prompt = jnp.stack(kv_p[1])
    new_kv2048_prompt = jnp.stack(kv_p[2])
    new_pool_prompt = jnp.stack(pool_p)
    new_kv128_sample = jnp.stack(kv_s[0])
    new_kv512_sample = jnp.stack(kv_s[1])
    new_kv2048_sample = jnp.stack(kv_s[2])
    new_pool_sample = jnp.stack(pool_s)
    return (y_prompt, y_sample, new_kv128_prompt, new_kv512_prompt, new_kv2048_prompt, new_pool_prompt,
            new_kv128_sample, new_kv512_sample, new_kv2048_sample, new_pool_sample)
```

```python
import functools

import jax
import jax.numpy as jnp
from jax import lax
from jax.experimental import pallas as pl
from jax.experimental.pallas import tpu as pltpu

F32 = jnp.float32
BF16 = jnp.bfloat16

D_MODEL = 1024
D_FF = 2816
POOL_WINDOWS = (2, 4, 8, 16)
POOL_GROUP_WIDTH = 128
POOL_WIDTH = 512
POOL_STATE = 15
ATT_CONFIGS = ((128, 1), (512, 4), (2048, 16))
N_GROUPS = 3
HEADS = 8
HEAD_DIM = 64
GROUP_WIDTH = HEADS * HEAD_DIM
ATT_WIDTH = N_GROUPS * GROUP_WIDTH
ATT_BLK = 128
EPS = 1e-6
Q_OFF = POOL_WIDTH
K_OFF = Q_OFF + ATT_WIDTH
V_OFF = K_OFF + ATT_WIDTH
GATE_OFF = V_OFF + ATT_WIDTH
IN_WIDTH = GATE_OFF + 2 * D_MODEL

FF_CHUNK = 256
VMEM_LIMIT = 56 * 1024 * 1024


def _params(n_axes):
    return pltpu.CompilerParams(
        dimension_semantics=("arbitrary",) * n_axes, vmem_limit_bytes=VMEM_LIMIT)


def _resident(shape):
    nd = len(shape)
    return pl.BlockSpec(shape, lambda *_: (0,) * nd, pipeline_mode=pl.Buffered(1))


def _rmsnorm(x, g):
    ms = jnp.mean(x * x, axis=-1, keepdims=True)
    return x * lax.rsqrt(ms + EPS) * g


def _mm(a, b):
    return jnp.dot(a, b, preferred_element_type=F32)


def _ffn_kernel(x_ref, g_ref, wgu_ref, wd_ref, o_ref, acc_ref):
    x = x_ref[...]
    xn = _rmsnorm(x, g_ref[...]).astype(BF16)
    for c in range(D_FF // FF_CHUNK):
        lo = c * FF_CHUNK
        a = _mm(xn, wgu_ref[:, lo:lo + FF_CHUNK])
        b = _mm(xn, wgu_ref[:, D_FF + lo:D_FF + lo + FF_CHUNK])
        h = (a * jax.nn.sigmoid(a) * b).astype(BF16)
        d = _mm(h, wd_ref[lo:lo + FF_CHUNK, :])
        if c == 0:
            acc_ref[...] = d
        else:
            acc_ref[...] += d
    o_ref[...] = x + 0.5 * acc_ref[...]


def _ffn(x, g, wgu, wd, tm):
    t = x.shape[0]
    return pl.pallas_call(
        _ffn_kernel,
        grid=(t // tm,),
        in_specs=[
            pl.BlockSpec((tm, D_MODEL), lambda i: (i, 0)),
            _resident((1, D_MODEL)),
            _resident((D_MODEL, 2 * D_FF)),
            _resident((D_FF, D_MODEL)),
        ],
        out_specs=pl.BlockSpec((tm, D_MODEL), lambda i: (i, 0)),
        out_shape=jax.ShapeDtypeStruct((t, D_MODEL), F32),
        scratch_shapes=[pltpu.VMEM((tm, D_MODEL), F32)],
        compiler_params=_params(1),
        name="ffn",
    )(x, g, wgu, wd)


def _head_rmsnorm(x, g):
    low = lax.broadcasted_iota(jnp.int32, (1, 2 * HEAD_DIM), 1) < HEAD_DIM
    outs = []
    for p in range(GROUP_WIDTH // (2 * HEAD_DIM)):
        xp = x[:, p * 2 * HEAD_DIM:(p + 1) * 2 * HEAD_DIM]
        sq = xp * xp
        s_all = jnp.sum(sq, axis=-1, keepdims=True)
        s_low = jnp.sum(jnp.where(low, sq, 0.0), axis=-1, keepdims=True)
        ms = jnp.where(low, s_low, s_all - s_low) * (1.0 / HEAD_DIM)
        outs.append(xp * lax.rsqrt(ms + EPS))
    return jnp.concatenate(outs, axis=-1) * g


def _in_kernel(x_ref, g_ref, w_ref, qn_ref, kn_ref,
               u_ref, gate_ref, qkv0_ref, qkv1_ref, qkv2_ref, kv0_ref, kv1_ref, kv2_ref,
               *, kv_rows):
    h = _rmsnorm(x_ref[...], g_ref[...]).astype(BF16)
    u_ref[...] = _mm(h, w_ref[:, 0:POOL_WIDTH])
    qkv_refs = (qkv0_ref, qkv1_ref, qkv2_ref)
    kv_refs = (kv0_ref, kv1_ref, kv2_ref)
    for g in range(N_GROUPS):
        c0, c1 = g * GROUP_WIDTH, (g + 1) * GROUP_WIDTH
        q = _head_rmsnorm(_mm(h, w_ref[:, Q_OFF + c0:Q_OFF + c1]), qn_ref[:, c0:c1])
        k = _head_rmsnorm(_mm(h, w_ref[:, K_OFF + c0:K_OFF + c1]), kn_ref[:, c0:c1])
        v = _mm(h, w_ref[:, V_OFF + c0:V_OFF + c1])
        qkv_refs[g][:, 0:GROUP_WIDTH] = q.astype(BF16)
        qkv_refs[g][:, GROUP_WIDTH:2 * GROUP_WIDTH] = k.astype(BF16)
        qkv_refs[g][:, 2 * GROUP_WIDTH:3 * GROUP_WIDTH] = v.astype(BF16)
        r0 = kv_rows[g]
        kv_refs[g][:, 0:GROUP_WIDTH] = k[r0:, :]
        kv_refs[g][:, GROUP_WIDTH:2 * GROUP_WIDTH] = v[r0:, :]
    gate_ref[...] = jax.nn.sigmoid(_mm(h, w_ref[:, GATE_OFF:IN_WIDTH]))


def _inproj(x1, g, w_in, qn, kn, n_seq, seq, tm, keeps):
    t = n_seq * seq
    nj = seq // tm
    kv_rows, kv_specs, kv_shapes = [], [], []
    for keep in keeps:
        if keep >= tm:
            nb = keep // tm
            kv_rows.append(0)
            kv_specs.append(pl.BlockSpec(
                (tm, 2 * GROUP_WIDTH),
                lambda b, j, nb=nb: (b * nb + jnp.maximum(j - (nj - nb), 0), 0)))
        else:
            kv_rows.append(tm - keep)
            kv_specs.append(pl.BlockSpec((keep, 2 * GROUP_WIDTH), lambda b, j: (b, 0)))
        kv_shapes.append(jax.ShapeDtypeStruct((n_seq * keep, 2 * GROUP_WIDTH), F32))
    row = lambda width: pl.BlockSpec((tm, width), lambda b, j: (b * nj + j, 0))
    return pl.pallas_call(
        functools.partial(_in_kernel, kv_rows=tuple(kv_rows)),
        grid=(n_seq, nj),
        in_specs=[
            row(D_MODEL),
            _resident((1, D_MODEL)),
            _resident((D_MODEL, IN_WIDTH)),
            _resident((1, ATT_WIDTH)),
            _resident((1, ATT_WIDTH)),
        ],
        out_specs=[row(POOL_WIDTH), row(2 * D_MODEL)] + [row(3 * GROUP_WIDTH)] * 3 + kv_specs,
        out_shape=[jax.ShapeDtypeStruct((t, POOL_WIDTH), F32),
                   jax.ShapeDtypeStruct((t, 2 * D_MODEL), F32)]
        + [jax.ShapeDtypeStruct((t, 3 * GROUP_WIDTH), BF16)] * 3 + kv_shapes,
        compiler_params=_params(2),
        name="inproj",
    )(x1, g, w_in, qn, kn)


def _softmax_av(s, v):
    mx = jnp.max(s, axis=-1, keepdims=True)
    e = jnp.exp(s - mx)
    den = jnp.sum(e, axis=-1, keepdims=True)
    p = (e * (1.0 / den)).astype(BF16)
    return _mm(p, v), mx + jnp.log(den)


def _qk(q, k):
    return lax.dot_general(q, k, (((1,), (1,)), ((), ())), preferred_element_type=F32) * (HEAD_DIM ** -0.5)


def _attn_kernel(qkv_ref, o_ref, l_ref, *, n_blocks):
    i = lax.broadcasted_iota(jnp.int32, (ATT_BLK, 2 * ATT_BLK), 0)
    j = lax.broadcasted_iota(jnp.int32, (ATT_BLK, 2 * ATT_BLK), 1)
    band = (j >= i) & (j <= i + ATT_BLK)
    first = band[:, ATT_BLK:]

    def block(q0, k0, n_keys, mask):
        for h in range(HEADS):
            c = h * HEAD_DIM
            q = qkv_ref[0, pl.ds(q0, ATT_BLK), c:c + HEAD_DIM]
            k = qkv_ref[0, pl.ds(k0, n_keys), GROUP_WIDTH + c:GROUP_WIDTH + c + HEAD_DIM]
            v = qkv_ref[0, pl.ds(k0, n_keys), 2 * GROUP_WIDTH + c:2 * GROUP_WIDTH + c + HEAD_DIM]
            o, lse = _softmax_av(jnp.where(mask, _qk(q, k), -jnp.inf), v)
            o_ref[0, pl.ds(q0, ATT_BLK), c:c + HEAD_DIM] = o
            l_ref[0, pl.ds(q0, ATT_BLK), c:c + HEAD_DIM] = jnp.broadcast_to(lse, (ATT_BLK, HEAD_DIM))

    block(0, 0, ATT_BLK, first)

    def body(n, carry):
        q0 = pl.multiple_of(n * ATT_BLK, ATT_BLK)
        block(q0, pl.multiple_of(q0 - ATT_BLK, ATT_BLK), 2 * ATT_BLK, band)
        return carry

    if n_blocks > 1:
        lax.fori_loop(1, n_blocks, body, 0)


def _attn_prompt(qkv, n_seq, seq, dil):
    ln = seq // dil
    view = qkv.reshape(n_seq, ln, dil * 3 * GROUP_WIDTH)
    out = jax.ShapeDtypeStruct((n_seq, ln, dil * GROUP_WIDTH), F32)
    ospec = pl.BlockSpec((1, ln, GROUP_WIDTH), lambda b, r: (b, 0, r))
    o, l = pl.pallas_call(
        functools.partial(_attn_kernel, n_blocks=ln // ATT_BLK),
        grid=(n_seq, dil),
        in_specs=[pl.BlockSpec((1, ln, 3 * GROUP_WIDTH), lambda b, r: (b, 0, r))],
        out_specs=[ospec, ospec],
        out_shape=[out, out],
        compiler_params=_params(2),
        name=f"attn_d{dil}",
    )(view)
    return o.reshape(n_seq * seq, GROUP_WIDTH), l.reshape(n_seq * seq, GROUP_WIDTH)


def _sattn_kernel(q_ref, kvn0_ref, kvn1_ref, kvn2_ref, c0_ref, c1_ref, c2_ref,
                  att_ref, n0_ref, n1_ref, n2_ref, *, t_new):
    rows = t_new * HEADS
    r_t = lax.broadcasted_iota(jnp.int32, (rows, 1), 0) // HEADS
    r_h = lax.broadcasted_iota(jnp.int32, (rows, GROUP_WIDTH), 0) % HEADS
    diag = (lax.broadcasted_iota(jnp.int32, (rows, GROUP_WIDTH), 1) // HEAD_DIM) == r_h
    groups = ((kvn0_ref, c0_ref, n0_ref), (kvn1_ref, c1_ref, n1_ref), (kvn2_ref, c2_ref, n2_ref))
    os_, ls = [], []
    for g, (kvn_ref, c_ref, n_ref) in enumerate(groups):
        dil = ATT_CONFIGS[g][1]
        w = c_ref.shape[1]
        kvn = kvn_ref[0]
        n_ref[0, 0:w - t_new, :] = c_ref[0, t_new:w, :]
        n_ref[0, w - t_new:w, :] = kvn
        kn = kvn[:, 0:GROUP_WIDTH].astype(BF16).astype(F32)
        vn = kvn[:, GROUP_WIDTH:2 * GROUP_WIDTH].astype(BF16).astype(F32)
        qg = q_ref[0, :, g * GROUP_WIDTH:(g + 1) * GROUP_WIDTH].astype(BF16).astype(F32)
        qf = jnp.concatenate(
            [jnp.broadcast_to(qg[t:t + 1, :], (HEADS, GROUP_WIDTH)) for t in range(t_new)], axis=0)
        qf = jnp.where(diag, qf, 0.0)
        kc = c_ref[0, :, 0:GROUP_WIDTH].astype(BF16)
        vc = c_ref[0, :, GROUP_WIDTH:2 * GROUP_WIDTH].astype(BF16)
        wi = lax.broadcasted_iota(jnp.int32, (rows, w), 1)
        ok_c = (wi >= r_t) & (((wi - r_t) & (dil - 1)) == 0)
        s_c = jnp.where(ok_c, _qk(qf.astype(BF16), kc), -jnp.inf)
        s_n = []
        for t2 in range(t_new):
            ok = (r_t >= t2) & (((r_t - t2) & (dil - 1)) == 0)
            s = jnp.sum(qf * kn[t2:t2 + 1, :], axis=-1, keepdims=True) * (HEAD_DIM ** -0.5)
            s_n.append(jnp.where(ok, s, -jnp.inf))
        mx = jnp.max(s_c, axis=-1, keepdims=True)
        for s in s_n:
            mx = jnp.maximum(mx, s)
        e_c = jnp.exp(s_c - mx)
        e_n = [jnp.exp(s - mx) for s in s_n]
        den = jnp.sum(e_c, axis=-1, keepdims=True)
        for e in e_n:
            den = den + e
        inv = 1.0 / den
        o = _mm((e_c * inv).astype(BF16), vc)
        for t2 in range(t_new):
            o = o + (e_n[t2] * inv).astype(BF16).astype(F32) * vn[t2:t2 + 1, :]
        os_.append(o)
        ls.append(mx + jnp.log(den))
    full = jnp.where(diag, _combine_groups(os_, ls), 0.0)
    att_ref[0] = jnp.concatenate(
        [jnp.sum(full[t * HEADS:(t + 1) * HEADS, :], axis=0, keepdims=True) for t in range(t_new)],
        axis=0)


def _attn_sample(q, kvns, caches, n_seq, t_new):
    q_v = q.reshape(n_seq, t_new, ATT_WIDTH)
    kvn_v = [a.reshape(n_seq, t_new, 2 * GROUP_WIDTH) for a in kvns]
    c_v = [c.reshape(n_seq, c.shape[1], 2 * GROUP_WIDTH) for c in caches]
    per_seq = lambda a: pl.BlockSpec((1,) + a.shape[1:], lambda b: (b, 0, 0))
    att = jax.ShapeDtypeStruct((n_seq, t_new, GROUP_WIDTH), F32)
    outs = pl.pallas_call(
        functools.partial(_sattn_kernel, t_new=t_new),
        grid=(n_seq,),
        in_specs=[per_seq(a) for a in [q_v] + kvn_v + c_v],
        out_specs=[per_seq(att)] + [per_seq(c) for c in c_v],
        out_shape=[att] + [jax.ShapeDtypeStruct(c.shape, F32) for c in c_v],
        compiler_params=_params(1),
        name="attn_sample",
    )(q_v, *kvn_v, *c_v)
    return outs[0], outs[1:]


def _combine_groups(os_, ls):
    mx = jnp.maximum(jnp.maximum(ls[0], ls[1]), ls[2])
    es = [jnp.exp(l - mx) for l in ls]
    inv = 1.0 / (es[0] + es[1] + es[2])
    return (es[0] * inv) * os_[0] + (es[1] * inv) * os_[1] + (es[2] * inv) * os_[2]


def _merge_tail(x1, pooled, att, gate, pw_ref, ps_ref, wbp_ref, wba_ref, wo_ref):
    mixed = jnp.concatenate(
        [_mm(pooled[gi].astype(BF16), pw_ref[gi]) for gi in range(len(POOL_WINDOWS))], axis=-1)
    pool_y = (mixed * ps_ref[...]).astype(BF16)
    merged = (gate[:, 0:D_MODEL] * _mm(pool_y, wbp_ref[...])
              + gate[:, D_MODEL:2 * D_MODEL] * _mm(att.astype(BF16), wba_ref[...]))
    return x1 + _mm(merged.astype(BF16), wo_ref[...])


def _merge_prompt_kernel(x1_ref, u_ref, up_ref, o0_ref, o1_ref, o2_ref, l0_ref, l1_ref, l2_ref,
                         gate_ref, pw_ref, ps_ref, wbp_ref, wba_ref, wo_ref, y_ref, ext_ref, *, tm):
    j = pl.program_id(1)
    u = u_ref[...]
    ext_ref[0:16, :] = jnp.where(j > 0, up_ref[...], 0.0)
    ext_ref[16:16 + tm, :] = u
    pos = j * tm + lax.broadcasted_iota(jnp.int32, (tm, 1), 0)
    pooled = []
    for gi, w in enumerate(POOL_WINDOWS):
        c0 = gi * POOL_GROUP_WIDTH
        acc = u[:, c0:c0 + POOL_GROUP_WIDTH]
        for k in range(1, w):
            acc = acc + ext_ref[16 - k:16 - k + tm, c0:c0 + POOL_GROUP_WIDTH]
        cnt = jnp.minimum(pos + 1, w).astype(F32)
        pooled.append(acc / cnt - u[:, c0:c0 + POOL_GROUP_WIDTH])
    att = _combine_groups((o0_ref[...], o1_ref[...], o2_ref[...]),
                          (l0_ref[...], l1_ref[...], l2_ref[...]))
    y_ref[...] = _merge_tail(x1_ref[...], pooled, att, gate_ref[...],
                             pw_ref, ps_ref, wbp_ref, wba_ref, wo_ref)


def _merge_prompt(x1, u, os_, ls, gate, pw, ps, wbp, wba, wo, n_seq, seq, tm):
    t = n_seq * seq
    nj = seq // tm
    row = lambda width: pl.BlockSpec((tm, width), lambda b, j: (b * nj + j, 0))
    prev = pl.BlockSpec((16, POOL_WIDTH),
                        lambda b, j: (jnp.maximum((b * nj + j) * (tm // 16) - 1, 0), 0))
    return pl.pallas_call(
        functools.partial(_merge_prompt_kernel, tm=tm),
        grid=(n_seq, nj),
        in_specs=[row(D_MODEL), row(POOL_WIDTH), prev] + [row(GROUP_WIDTH)] * 6 + [row(2 * D_MODEL)]
        + [_resident(a.shape) for a in (pw, ps, wbp, wba, wo)],
        out_specs=row(D_MODEL),
        out_shape=jax.ShapeDtypeStruct((t, D_MODEL), F32),
        scratch_shapes=[pltpu.VMEM((tm + 16, POOL_WIDTH), F32)],
        compiler_params=_params(2),
        name="merge_prompt",
    )(x1, u, u, *os_, *ls, gate, pw, ps, wbp, wba, wo)


def _merge_sample_kernel(x1_ref, ctx_ref, att_ref, gate_ref, pw_ref, ps_ref, wbp_ref, wba_ref,
                         wo_ref, y_ref, *, t_new):
    n_seq = ctx_ref.shape[0]
    for t in range(t_new):
        def ctx_row(r, c0):
            return ctx_ref[:, r * POOL_WIDTH + c0:r * POOL_WIDTH + c0 + POOL_GROUP_WIDTH]
        pooled = []
        for gi, w in enumerate(POOL_WINDOWS):
            c0 = gi * POOL_GROUP_WIDTH
            last = POOL_STATE + t
            acc = ctx_row(last, c0)
            for k in range(1, w):
                acc = acc + ctx_row(last - k, c0)
            pooled.append(acc / float(min(w, last + 1)) - ctx_row(last, c0))
        rows = pl.ds(t * n_seq, n_seq)
        y_ref[rows, :] = _merge_tail(x1_ref[rows, :], pooled, att_ref[rows, :], gate_ref[rows, :],
                                     pw_ref, ps_ref, wbp_ref, wba_ref, wo_ref)


def _merge_sample(x1, ctx, att, gate, pw, ps, wbp, wba, wo, t_new):
    args = (x1, ctx, att, gate, pw, ps, wbp, wba, wo)
    return pl.pallas_call(
        functools.partial(_merge_sample_kernel, t_new=t_new),
        grid=(1,),
        in_specs=[pl.BlockSpec(a.shape, lambda i, nd=a.ndim: (0,) * nd) for a in args],
        out_specs=pl.BlockSpec(x1.shape, lambda i: (0, 0)),
        out_shape=jax.ShapeDtypeStruct(x1.shape, F32),
        compiler_params=_params(1),
        name="merge_sample",
    )(*args)


def kernel(x_prompt, x_sample, cache_kv_w128, cache_kv_w512, cache_kv_w2048, state_pool,
           ffn1_norm, ffn1_w_gu, ffn1_w_down, mix_norm, w_in, q_norm, k_norm, pool_w,
           pool_scale, w_branch_pool, w_branch_att, w_out, ffn2_norm, ffn2_w_gu, ffn2_w_down):
    depth = ffn1_norm.shape[0]
    assert depth == 1
    n_p, seq, _ = x_prompt.shape
    n_s, t_new, _ = x_sample.shape
    caches = (cache_kv_w128[0], cache_kv_w512[0], cache_kv_w2048[0])

    g1 = ffn1_norm[0][None, :]
    g2 = ffn2_norm[0][None, :]
    gm = mix_norm[0][None, :]
    wgu1, wd1 = ffn1_w_gu[0].astype(BF16), ffn1_w_down[0].astype(BF16)
    wgu2, wd2 = ffn2_w_gu[0].astype(BF16), ffn2_w_down[0].astype(BF16)
    win = w_in[0].astype(BF16)
    qn = q_norm[0].reshape(1, ATT_WIDTH)
    kn = k_norm[0].reshape(1, ATT_WIDTH)
    pw = pool_w[0].astype(BF16)
    ps = pool_scale[0][None, :]
    wbp, wba, wo = (w_branch_pool[0].astype(BF16), w_branch_att[0].astype(BF16), w_out[0].astype(BF16))

    xp = x_prompt.reshape(n_p * seq, D_MODEL)
    x1 = _ffn(xp, g1, wgu1, wd1, tm=512)
    keeps = tuple(min(w, seq) for w, _ in ATT_CONFIGS)
    u, gate, qkv0, qkv1, qkv2, kv0, kv1, kv2 = _inproj(x1, gm, win, qn, kn, n_p, seq, 256, keeps)
    os_, ls = [], []
    for qkv, (_, dil) in zip((qkv0, qkv1, qkv2), ATT_CONFIGS):
        o, l = _attn_prompt(qkv, n_p, seq, dil)
        os_.append(o)
        ls.append(l)
    x2 = _merge_prompt(x1, u, os_, ls, gate, pw, ps, wbp, wba, wo, n_p, seq, tm=256)
    y_prompt = _ffn(x2, g2, wgu2, wd2, tm=512).reshape(n_p, seq, D_MODEL)
    new_kv_p = [kv.reshape(1, n_p, min(w, seq), 2, HEADS, HEAD_DIM)
                for kv, (w, _) in zip((kv0, kv1, kv2), ATT_CONFIGS)]
    new_pool_p = u.reshape(n_p, seq, POOL_WIDTH)[None, :, seq - POOL_STATE:, :]

    n_tok = n_s * t_new
    xs = x_sample.reshape(n_tok, D_MODEL)
    x1s = _ffn(xs, g1, wgu1, wd1, tm=n_tok)
    us, gates, sq0, sq1, sq2, sk0, sk1, sk2 = _inproj(
        x1s, gm, win, qn, kn, 1, n_tok, 256, (n_tok,) * N_GROUPS)
    q_s = jnp.concatenate([a[:, 0:GROUP_WIDTH] for a in (sq0, sq1, sq2)], axis=-1).astype(F32)
    att_s, new_caches = _attn_sample(q_s, (sk0, sk1, sk2), caches, n_s, t_new)
    u_ctx = jnp.concatenate([state_pool[0], us.reshape(n_s, t_new, POOL_WIDTH)], axis=1)
    new_pool_s = u_ctx[None, :, t_new:, :]
    to_ts = lambda a: a.reshape(n_s, t_new, -1).transpose(1, 0, 2).reshape(n_tok, -1)
    x2s = _merge_sample(to_ts(x1s), u_ctx.reshape(n_s, (POOL_STATE + t_new) * POOL_WIDTH),
                        to_ts(att_s), to_ts(gates), pw, ps, wbp, wba, wo, t_new)
    y_s = _ffn(x2s, g2, wgu2, wd2, tm=n_tok)
    y_sample = y_s.reshape(t_new, n_s, D_MODEL).transpose(1, 0, 2)
    new_kv_s = [c.reshape(1, n_s, c.shape[1], 2, HEADS, HEAD_DIM) for c in new_caches]

    return (y_prompt, y_sample, new_kv_p[0], new_kv_p[1], new_kv_p[2], new_pool_p,
            new_kv_s[0], new_kv_s[1], new_kv_s[2], new_pool_s)
```

```python
import functools

import jax
import jax.numpy as jnp
from jax import lax
from jax.experimental import pallas as pl
from jax.experimental.pallas import tpu as pltpu

F32 = jnp.float32
BF16 = jnp.bfloat16

D_MODEL = 1024
D_FF = 2816
POOL_WINDOWS = (2, 4, 8, 16)
POOL_GROUP_WIDTH = 128
POOL_WIDTH = 512
POOL_STATE = 15
ATT_CONFIGS = ((128, 1), (512, 4), (2048, 16))
N_GROUPS = 3
HEADS = 8
HEAD_DIM = 64
GROUP_WIDTH = HEADS * HEAD_DIM
ATT_WIDTH = N_GROUPS * GROUP_WIDTH
ATT_BLK = 128
EPS = 1e-6
Q_OFF = POOL_WIDTH
K_OFF = Q_OFF + ATT_WIDTH
V_OFF = K_OFF + ATT_WIDTH
GATE_OFF = V_OFF + ATT_WIDTH
IN_WIDTH = GATE_OFF + 2 * D_MODEL

LANES = 128
FF_CHUNK = 256
VMEM_LIMIT = 56 * 1024 * 1024


def _params(n_axes, vmem=VMEM_LIMIT):
    return pltpu.CompilerParams(
        dimension_semantics=("arbitrary",) * n_axes, vmem_limit_bytes=vmem)


def _resident(shape):
    nd = len(shape)
    return pl.BlockSpec(shape, lambda *_: (0,) * nd, pipeline_mode=pl.Buffered(1))


def _rmsnorm(x, g):
    ms = jnp.mean(x * x, axis=-1, keepdims=True)
    return x * lax.rsqrt(ms + EPS) * g


def _mm(a, b):
    return jnp.dot(a, b, preferred_element_type=F32)


def _mm_nt(a, b):
    return lax.dot_general(a, b, (((1,), (1,)), ((), ())), preferred_element_type=F32)


def _ffn_kernel(x_ref, g_ref, wgu_ref, wd_ref, o_ref, acc_ref):
    x = x_ref[...]
    xn = _rmsnorm(x, g_ref[...]).astype(BF16)
    for c in range(D_FF // FF_CHUNK):
        lo = c * FF_CHUNK
        a = _mm(xn, wgu_ref[:, lo:lo + FF_CHUNK])
        b = _mm(xn, wgu_ref[:, D_FF + lo:D_FF + lo + FF_CHUNK])
        h = (a * jax.nn.sigmoid(a) * b).astype(BF16)
        d = _mm(h, wd_ref[lo:lo + FF_CHUNK, :])
        if c == 0:
            acc_ref[...] = d
        else:
            acc_ref[...] += d
    o_ref[...] = x + 0.5 * acc_ref[...]


def _ffn(x, g, wgu, wd, tm):
    t = x.shape[0]
    return pl.pallas_call(
        _ffn_kernel,
        grid=(t // tm,),
        in_specs=[
            pl.BlockSpec((tm, D_MODEL), lambda i: (i, 0)),
            _resident((1, D_MODEL)),
            _resident((D_MODEL, 2 * D_FF)),
            _resident((D_FF, D_MODEL)),
        ],
        out_specs=pl.BlockSpec((tm, D_MODEL), lambda i: (i, 0)),
        out_shape=jax.ShapeDtypeStruct((t, D_MODEL), F32),
        scratch_shapes=[pltpu.VMEM((tm, D_MODEL), F32)],
        compiler_params=_params(1),
        name="ffn",
    )(x, g, wgu, wd)


def _head_rmsnorm(x, g):
    low = lax.broadcasted_iota(jnp.int32, (1, LANES), 1) < HEAD_DIM
    outs = []
    for p in range(GROUP_WIDTH // LANES):
        xp = x[:, p * LANES:(p + 1) * LANES]
        sq = xp * xp
        s_all = jnp.sum(sq, axis=-1, keepdims=True)
        s_low = jnp.sum(jnp.where(low, sq, 0.0), axis=-1, keepdims=True)
        ms = jnp.where(low, s_low, s_all - s_low) * (1.0 / HEAD_DIM)
        outs.append(xp * lax.rsqrt(ms + EPS))
    return jnp.concatenate(outs, axis=-1) * g


def _in_kernel(x_ref, g_ref, w_ref, qn_ref, kn_ref, *refs, tm, dils, kv_from, kv_rows, row_kv):
    u_ref, gate_ref = refs[0:2]
    qkv_refs = refs[2:5]
    kvt_refs = refs[5:8]
    kvrow_refs = refs[8:11] if row_kv else None
    stage_ref = refs[-1]
    j = pl.program_id(1)
    h = _rmsnorm(x_ref[...], g_ref[...]).astype(BF16)
    u_ref[...] = _mm(h, w_ref[:, 0:POOL_WIDTH])
    for g in range(N_GROUPS):
        c0, c1 = g * GROUP_WIDTH, (g + 1) * GROUP_WIDTH
        q = _head_rmsnorm(_mm(h, w_ref[:, Q_OFF + c0:Q_OFF + c1]), qn_ref[:, c0:c1])
        k = _head_rmsnorm(_mm(h, w_ref[:, K_OFF + c0:K_OFF + c1]), kn_ref[:, c0:c1])
        v = _mm(h, w_ref[:, V_OFF + c0:V_OFF + c1])
        d = dils[g]
        if d == 1:
            qkv_refs[g][:, 0:GROUP_WIDTH] = q.astype(BF16)
            qkv_refs[g][:, GROUP_WIDTH:2 * GROUP_WIDTH] = k.astype(BF16)
            qkv_refs[g][:, 2 * GROUP_WIDTH:3 * GROUP_WIDTH] = v.astype(BF16)
        else:
            for ci, a in enumerate((q, k, v)):
                for cl in range(GROUP_WIDTH // LANES):
                    stage_ref[ci * (GROUP_WIDTH // LANES) + cl] = a[:, cl * LANES:(cl + 1) * LANES]
            for r in range(d):
                for cl in range(3 * GROUP_WIDTH // LANES):
                    qkv_refs[g][0, r, :, cl * LANES:(cl + 1) * LANES] = (
                        stage_ref[cl, pl.ds(r, tm // d, stride=d), :].astype(BF16))
        if row_kv:
            kvrow_refs[g][:, 0:GROUP_WIDTH] = k
            kvrow_refs[g][:, GROUP_WIDTH:2 * GROUP_WIDTH] = v

        @pl.when(j >= kv_from[g])
        def _(k=k, v=v, g=g):
            r0 = kv_rows[g]
            kvt_refs[g][0, 0:GROUP_WIDTH, :] = k[r0:, :].T
            kvt_refs[g][0, GROUP_WIDTH:2 * GROUP_WIDTH, :] = v[r0:, :].T
    gate_ref[...] = jax.nn.sigmoid(_mm(h, w_ref[:, GATE_OFF:IN_WIDTH]))


def _inproj(x1, g, w_in, qn, kn, n_seq, seq, tm, keeps, dils, row_kv):
    t = n_seq * seq
    nj = seq // tm
    row = lambda width: pl.BlockSpec((tm, width), lambda b, j: (b * nj + j, 0))
    qkv_specs, qkv_shapes = [], []
    for d in dils:
        if d == 1:
            qkv_specs.append(row(3 * GROUP_WIDTH))
            qkv_shapes.append(jax.ShapeDtypeStruct((t, 3 * GROUP_WIDTH), BF16))
        else:
            qkv_specs.append(pl.BlockSpec((1, d, tm // d, 3 * GROUP_WIDTH), lambda b, j: (b, 0, j, 0)))
            qkv_shapes.append(jax.ShapeDtypeStruct((n_seq, d, seq // d, 3 * GROUP_WIDTH), BF16))
    kv_from, kv_rows, kvt_specs, kvt_shapes = [], [], [], []
    for keep in keeps:
        tw = min(tm, keep)
        nb = keep // tw
        kv_from.append(nj - nb)
        kv_rows.append(tm - tw)
        kvt_specs.append(pl.BlockSpec(
            (1, 2 * GROUP_WIDTH, tw), lambda b, j, nb=nb: (b, 0, jnp.maximum(j - (nj - nb), 0))))
        kvt_shapes.append(jax.ShapeDtypeStruct((n_seq, 2 * GROUP_WIDTH, keep), F32))
    out_specs = [row(POOL_WIDTH), row(2 * D_MODEL)] + qkv_specs + kvt_specs
    out_shape = [jax.ShapeDtypeStruct((t, POOL_WIDTH), F32),
                 jax.ShapeDtypeStruct((t, 2 * D_MODEL), F32)] + qkv_shapes + kvt_shapes
    if row_kv:
        out_specs += [row(2 * GROUP_WIDTH)] * N_GROUPS
        out_shape += [jax.ShapeDtypeStruct((t, 2 * GROUP_WIDTH), F32)] * N_GROUPS
    return pl.pallas_call(
        functools.partial(_in_kernel, tm=tm, dils=tuple(dils), kv_from=tuple(kv_from),
                          kv_rows=tuple(kv_rows), row_kv=row_kv),
        grid=(n_seq, nj),
        in_specs=[
            row(D_MODEL),
            _resident((1, D_MODEL)),
            _resident((D_MODEL, IN_WIDTH)),
            _resident((1, ATT_WIDTH)),
            _resident((1, ATT_WIDTH)),
        ],
        out_specs=out_specs,
        out_shape=out_shape,
        scratch_shapes=[pltpu.VMEM((3 * GROUP_WIDTH // LANES, tm, LANES), F32)],
        compiler_params=_params(2),
        name="inproj",
    )(x1, g, w_in, qn, kn)


def _attn_kernel(qkv_ref, o_ref, l_ref, *, n_blocks):
    n_streams = qkv_ref.shape[1]
    i = lax.broadcasted_iota(jnp.int32, (ATT_BLK, 2 * ATT_BLK), 0)
    j = lax.broadcasted_iota(jnp.int32, (ATT_BLK, 2 * ATT_BLK), 1)
    band = (j >= i) & (j <= i + ATT_BLK)
    first = band[:, ATT_BLK:]
    low = lax.broadcasted_iota(jnp.int32, (ATT_BLK, LANES), 1) < HEAD_DIM
    scale = jnp.asarray(HEAD_DIM ** -0.5, BF16)

    def block(r, q0, k0, n_keys, mask):
        for p in range(GROUP_WIDTH // LANES):
            c = p * LANES
            qp = qkv_ref[0, r, pl.ds(q0, ATT_BLK), c:c + LANES] * scale
            kp = qkv_ref[0, r, pl.ds(k0, n_keys), GROUP_WIDTH + c:GROUP_WIDTH + c + LANES]
            vp = qkv_ref[0, r, pl.ds(k0, n_keys), 2 * GROUP_WIDTH + c:2 * GROUP_WIDTH + c + LANES]
            halves = []
            for own in (low, ~low):
                s = jnp.where(mask, _mm_nt(jnp.where(own, qp, jnp.zeros_like(qp)), kp), -jnp.inf)
                mx = jnp.max(s, axis=-1, keepdims=True)
                e = jnp.exp(s - mx)
                den = jnp.sum(e, axis=-1, keepdims=True)
                halves.append((_mm(e.astype(BF16), vp) * (1.0 / den), mx + jnp.log(den)))
            o_ref[0, r, pl.ds(q0, ATT_BLK), c:c + LANES] = jnp.where(low, halves[0][0], halves[1][0])
            l_ref[0, r, pl.ds(q0, ATT_BLK), c:c + LANES] = jnp.where(low, halves[0][1], halves[1][1])

    def body(it, carry):
        if n_blocks == 1:
            block(it, 0, 0, ATT_BLK, first)
        else:
            r = it // n_blocks
            n = it % n_blocks

            @pl.when(n == 0)
            def _():
                block(r, 0, 0, ATT_BLK, first)

            @pl.when(n > 0)
            def _():
                q0 = pl.multiple_of(n * ATT_BLK, ATT_BLK)
                block(r, q0, pl.multiple_of(q0 - ATT_BLK, ATT_BLK), 2 * ATT_BLK, band)
        return carry

    lax.fori_loop(0, n_streams * n_blocks, body, 0)


def _attn_prompt(qkv, n_seq, seq, dil):
    ln = seq // dil
    out = jax.ShapeDtypeStruct((n_seq, dil, ln, GROUP_WIDTH), F32)
    ospec = pl.BlockSpec((1, dil, ln, GROUP_WIDTH), lambda b: (b, 0, 0, 0))
    return pl.pallas_call(
        functools.partial(_attn_kernel, n_blocks=ln // ATT_BLK),
        grid=(n_seq,),
        in_specs=[pl.BlockSpec((1, dil, ln, 3 * GROUP_WIDTH), lambda b: (b, 0, 0, 0))],
        out_specs=[ospec, ospec],
        out_shape=[out, out],
        compiler_params=_params(1),
        name=f"attn_d{dil}",
    )(qkv)


def _sattn_kernel(q_ref, kvn0_ref, kvn1_ref, kvn2_ref, kvt0_ref, kvt1_ref, kvt2_ref,
                  c0_ref, c1_ref, c2_ref, att_ref, n0_ref, n1_ref, n2_ref, *, t_new, seq_per_tile):
    b = pl.program_id(0)
    rows = t_new * HEADS
    r_t = lax.broadcasted_iota(jnp.int32, (rows, 1), 0) // HEADS
    r_h = lax.broadcasted_iota(jnp.int32, (rows, GROUP_WIDTH), 0) % HEADS
    diag = (lax.broadcasted_iota(jnp.int32, (rows, GROUP_WIDTH), 1) // HEAD_DIM) == r_h
    tail_lane = lax.broadcasted_iota(jnp.int32, (2 * GROUP_WIDTH, LANES), 1) >= LANES - t_new
    tail_shift = (LANES - t_new) - t_new * (b % seq_per_tile)
    groups = ((kvn0_ref, kvt0_ref, c0_ref, n0_ref), (kvn1_ref, kvt1_ref, c1_ref, n1_ref),
              (kvn2_ref, kvt2_ref, c2_ref, n2_ref))
    os_, ls = [], []
    for g, (kvn_ref, kvt_ref, c_ref, n_ref) in enumerate(groups):
        dil = ATT_CONFIGS[g][1]
        w = c_ref.shape[2]
        n_ref[0] = pltpu.roll(c_ref[0], w - t_new, axis=1)
        new_cols = pltpu.roll(kvt_ref[...], tail_shift, axis=1)
        n_ref[0, :, w - LANES:w] = jnp.where(tail_lane, new_cols, n_ref[0, :, w - LANES:w])
        kvn = kvn_ref[0]
        kn = kvn[:, 0:GROUP_WIDTH].astype(BF16).astype(F32)
        vn = kvn[:, GROUP_WIDTH:2 * GROUP_WIDTH].astype(BF16).astype(F32)
        qg = q_ref[0, :, g * GROUP_WIDTH:(g + 1) * GROUP_WIDTH].astype(BF16).astype(F32)
        qf = jnp.concatenate(
            [jnp.broadcast_to(qg[t:t + 1, :], (HEADS, GROUP_WIDTH)) for t in range(t_new)], axis=0)
        qf = jnp.where(diag, qf, 0.0)
        kc = c_ref[0, 0:GROUP_WIDTH, :].astype(BF16)
        vc = c_ref[0, GROUP_WIDTH:2 * GROUP_WIDTH, :].astype(BF16)
        wi = lax.broadcasted_iota(jnp.int32, (rows, w), 1)
        ok_c = (wi >= r_t) & (((wi - r_t) & (dil - 1)) == 0)
        s_c = jnp.where(ok_c, _mm(qf.astype(BF16), kc) * (HEAD_DIM ** -0.5), -jnp.inf)
        s_n = []
        for t2 in range(t_new):
            ok = (r_t >= t2) & (((r_t - t2) & (dil - 1)) == 0)
            s = jnp.sum(qf * kn[t2:t2 + 1, :], axis=-1, keepdims=True) * (HEAD_DIM ** -0.5)
            s_n.append(jnp.where(ok, s, -jnp.inf))
        mx = jnp.max(s_c, axis=-1, keepdims=True)
        for s in s_n:
            mx = jnp.maximum(mx, s)
        e_c = jnp.exp(s_c - mx)
        e_n = [jnp.exp(s - mx) for s in s_n]
        den = jnp.sum(e_c, axis=-1, keepdims=True)
        for e in e_n:
            den = den + e
        inv = 1.0 / den
        o = _mm_nt((e_c * inv).astype(BF16), vc)
        for t2 in range(t_new):
            o = o + (e_n[t2] * inv).astype(BF16).astype(F32) * vn[t2:t2 + 1, :]
        os_.append(o)
        ls.append(mx + jnp.log(den))
    full = jnp.where(diag, _combine_groups(os_, ls), 0.0)
    att_ref[0] = jnp.concatenate(
        [jnp.sum(full[t * HEADS:(t + 1) * HEADS, :], axis=0, keepdims=True) for t in range(t_new)],
        axis=0)


def _attn_sample(q, kvns, kvts, caches_t, n_seq, t_new):
    seq_per_tile = LANES // t_new
    q_v = q.reshape(n_seq, t_new, ATT_WIDTH)
    kvn_v = [a.reshape(n_seq, t_new, 2 * GROUP_WIDTH) for a in kvns]
    per_seq = lambda a: pl.BlockSpec((1,) + a.shape[1:], lambda b: (b, 0, 0))
    kvt_spec = pl.BlockSpec((2 * GROUP_WIDTH, LANES), lambda b: (0, b // seq_per_tile))
    att = jax.ShapeDtypeStruct((n_seq, t_new, GROUP_WIDTH), F32)
    outs = pl.pallas_call(
        functools.partial(_sattn_kernel, t_new=t_new, seq_per_tile=seq_per_tile),
        grid=(n_seq,),
        in_specs=[per_seq(a) for a in [q_v] + kvn_v] + [kvt_spec] * N_GROUPS
        + [per_seq(c) for c in caches_t],
        out_specs=[per_seq(att)] + [per_seq(c) for c in caches_t],
        out_shape=[att] + [jax.ShapeDtypeStruct(c.shape, F32) for c in caches_t],
        compiler_params=_params(1, vmem=60 * 1024 * 1024),
        name="attn_sample",
    )(q_v, *kvn_v, *kvts, *caches_t)
    return outs[0], outs[1:]


def _combine_groups(os_, ls):
    mx = jnp.maximum(jnp.maximum(ls[0], ls[1]), ls[2])
    es = [jnp.exp(l - mx) for l in ls]
    inv = 1.0 / (es[0] + es[1] + es[2])
    return (es[0] * inv) * os_[0] + (es[1] * inv) * os_[1] + (es[2] * inv) * os_[2]


def _merge_tail(x1, pooled, att, gate, pw_ref, ps_ref, wbp_ref, wba_ref, wo_ref):
    mixed = jnp.concatenate(
        [_mm(pooled[gi].astype(BF16), pw_ref[gi]) for gi in range(len(POOL_WINDOWS))], axis=-1)
    pool_y = (mixed * ps_ref[...]).astype(BF16)
    merged = (gate[:, 0:D_MODEL] * _mm(pool_y, wbp_ref[...])
              + gate[:, D_MODEL:2 * D_MODEL] * _mm(att.astype(BF16), wba_ref[...]))
    return x1 + _mm(merged.astype(BF16), wo_ref[...])


def _merge_prompt_kernel(x1_ref, u_ref, up_ref, o0_ref, o1_ref, o2_ref, l0_ref, l1_ref, l2_ref,
                         gate_ref, pw_ref, ps_ref, wbp_ref, wba_ref, wo_ref, y_ref,
                         ext_ref, nat_ref, *, tm):
    j = pl.program_id(1)
    u = u_ref[...]
    ext_ref[0:16, :] = jnp.where(j > 0, up_ref[...], 0.0)
    ext_ref[16:16 + tm, :] = u
    pos = j * tm + lax.broadcasted_iota(jnp.int32, (tm, 1), 0)
    pooled = []
    for gi, w in enumerate(POOL_WINDOWS):
        c0 = gi * POOL_GROUP_WIDTH
        acc = u[:, c0:c0 + POOL_GROUP_WIDTH]
        for k in range(1, w):
            acc = acc + ext_ref[16 - k:16 - k + tm, c0:c0 + POOL_GROUP_WIDTH]
        cnt = jnp.minimum(pos + 1, w).astype(F32)
        pooled.append(acc / cnt - u[:, c0:c0 + POOL_GROUP_WIDTH])

    def natural(ref, slot):
        if len(ref.shape) == 2:
            return ref[...]
        d = ref.shape[1]
        n_tiles = GROUP_WIDTH // LANES
        for r in range(d):
            for cl in range(n_tiles):
                nat_ref[slot * n_tiles + cl, pl.ds(r, tm // d, stride=d), :] = (
                    ref[0, r, :, cl * LANES:(cl + 1) * LANES])
        return jnp.concatenate([nat_ref[slot * n_tiles + cl] for cl in range(n_tiles)], axis=-1)

    att = _combine_groups((natural(o0_ref, 0), natural(o1_ref, 1), natural(o2_ref, 2)),
                          (natural(l0_ref, 3), natural(l1_ref, 4), natural(l2_ref, 5)))
    y_ref[...] = _merge_tail(x1_ref[...], pooled, att, gate_ref[...],
                             pw_ref, ps_ref, wbp_ref, wba_ref, wo_ref)


def _merge_prompt(x1, u, os_, ls, gate, pw, ps, wbp, wba, wo, n_seq, seq, tm):
    t = n_seq * seq
    nj = seq // tm
    row = lambda width: pl.BlockSpec((tm, width), lambda b, j: (b * nj + j, 0))
    prev = pl.BlockSpec((16, POOL_WIDTH),
                        lambda b, j: (jnp.maximum((b * nj + j) * (tm // 16) - 1, 0), 0))
    att_args, att_specs = [], []
    for a in list(os_) + list(ls):
        d = a.shape[1]
        if d == 1:
            att_args.append(a.reshape(t, GROUP_WIDTH))
            att_specs.append(row(GROUP_WIDTH))
        else:
            att_args.append(a)
            att_specs.append(pl.BlockSpec((1, d, tm // d, GROUP_WIDTH), lambda b, j: (b, 0, j, 0)))
    return pl.pallas_call(
        functools.partial(_merge_prompt_kernel, tm=tm),
        grid=(n_seq, nj),
        in_specs=[row(D_MODEL), row(POOL_WIDTH), prev] + att_specs + [row(2 * D_MODEL)]
        + [_resident(a.shape) for a in (pw, ps, wbp, wba, wo)],
        out_specs=row(D_MODEL),
        out_shape=jax.ShapeDtypeStruct((t, D_MODEL), F32),
        scratch_shapes=[pltpu.VMEM((tm + 16, POOL_WIDTH), F32),
                        pltpu.VMEM((2 * N_GROUPS * GROUP_WIDTH // LANES, tm, LANES), F32)],
        compiler_params=_params(2),
        name="merge_prompt",
    )(x1, u, u, *att_args, gate, pw, ps, wbp, wba, wo)


def _merge_sample_kernel(x1_ref, ctx_ref, att_ref, gate_ref, pw_ref, ps_ref, wbp_ref, wba_ref,
                         wo_ref, y_ref, *, t_new):
    n_seq = ctx_ref.shape[0]
    for t in range(t_new):
        def ctx_row(r, c0):
            return ctx_ref[:, r * POOL_WIDTH + c0:r * POOL_WIDTH + c0 + POOL_GROUP_WIDTH]
        pooled = []
        for gi, w in enumerate(POOL_WINDOWS):
            c0 = gi * POOL_GROUP_WIDTH
            last = POOL_STATE + t
            acc = ctx_row(last, c0)
            for k in range(1, w):
                acc = acc + ctx_row(last - k, c0)
            pooled.append(acc / float(min(w, last + 1)) - ctx_row(last, c0))
        rows = pl.ds(t * n_seq, n_seq)
        y_ref[rows, :] = _merge_tail(x1_ref[rows, :], pooled, att_ref[rows, :], gate_ref[rows, :],
                                     pw_ref, ps_ref, wbp_ref, wba_ref, wo_ref)


def _merge_sample(x1, ctx, att, gate, pw, ps, wbp, wba, wo, t_new):
    args = (x1, ctx, att, gate, pw, ps, wbp, wba, wo)
    return pl.pallas_call(
        functools.partial(_merge_sample_kernel, t_new=t_new),
        grid=(1,),
        in_specs=[pl.BlockSpec(a.shape, lambda i, nd=a.ndim: (0,) * nd) for a in args],
        out_specs=pl.BlockSpec(x1.shape, lambda i: (0, 0)),
        out_shape=jax.ShapeDtypeStruct(x1.shape, F32),
        compiler_params=_params(1),
        name="merge_sample",
    )(*args)


def _cache_to_cols(c):
    n_seq, w = c.shape[0], c.shape[1]
    return c.transpose(0, 2, 3, 4, 1).reshape(n_seq, 2 * GROUP_WIDTH, w)


def _cols_to_cache(ct):
    n_seq, _, w = ct.shape
    return ct.reshape(n_seq, 2, HEADS, HEAD_DIM, w).transpose(0, 4, 1, 2, 3)[None]


def kernel(x_prompt, x_sample, cache_kv_w128, cache_kv_w512, cache_kv_w2048, state_pool,
           ffn1_norm, ffn1_w_gu, ffn1_w_down, mix_norm, w_in, q_norm, k_norm, pool_w,
           pool_scale, w_branch_pool, w_branch_att, w_out, ffn2_norm, ffn2_w_gu, ffn2_w_down):
    depth = ffn1_norm.shape[0]
    assert depth == 1
    n_p, seq, _ = x_prompt.shape
    n_s, t_new, _ = x_sample.shape
    caches_t = [_cache_to_cols(c[0]) for c in (cache_kv_w128, cache_kv_w512, cache_kv_w2048)]
    dils = tuple(d for _, d in ATT_CONFIGS)

    g1 = ffn1_norm[0][None, :]
    g2 = ffn2_norm[0][None, :]
    gm = mix_norm[0][None, :]
    wgu1, wd1 = ffn1_w_gu[0].astype(BF16), ffn1_w_down[0].astype(BF16)
    wgu2, wd2 = ffn2_w_gu[0].astype(BF16), ffn2_w_down[0].astype(BF16)
    win = w_in[0].astype(BF16)
    qn = q_norm[0].reshape(1, ATT_WIDTH)
    kn = k_norm[0].reshape(1, ATT_WIDTH)
    pw = pool_w[0].astype(BF16)
    ps = pool_scale[0][None, :]
    wbp, wba, wo = (w_branch_pool[0].astype(BF16), w_branch_att[0].astype(BF16), w_out[0].astype(BF16))

    xp = x_prompt.reshape(n_p * seq, D_MODEL)
    x1 = _ffn(xp, g1, wgu1, wd1, tm=512)
    keeps = tuple(min(w, seq) for w, _ in ATT_CONFIGS)
    u, gate, qkv0, qkv1, qkv2, kvt0, kvt1, kvt2 = _inproj(
        x1, gm, win, qn, kn, n_p, seq, 256, keeps, dils, row_kv=False)
    os_, ls = [], []
    for qkv, dil in zip((qkv0, qkv1, qkv2), dils):
        o, l = _attn_prompt(qkv.reshape(n_p, dil, seq // dil, 3 * GROUP_WIDTH), n_p, seq, dil)
        os_.append(o)
        ls.append(l)
    x2 = _merge_prompt(x1, u, os_, ls, gate, pw, ps, wbp, wba, wo, n_p, seq, tm=256)
    y_prompt = _ffn(x2, g2, wgu2, wd2, tm=512).reshape(n_p, seq, D_MODEL)
    new_kv_p = [_cols_to_cache(kvt) for kvt in (kvt0, kvt1, kvt2)]
    new_pool_p = u.reshape(n_p, seq, POOL_WIDTH)[None, :, seq - POOL_STATE:, :]

    n_tok = n_s * t_new
    xs = x_sample.reshape(n_tok, D_MODEL)
    x1s = _ffn(xs, g1, wgu1, wd1, tm=n_tok)
    us, gates, sq0, sq1, sq2, st0, st1, st2, sk0, sk1, sk2 = _inproj(
        x1s, gm, win, qn, kn, 1, n_tok, 256, (n_tok,) * N_GROUPS, (1,) * N_GROUPS, row_kv=True)
    q_s = jnp.concatenate([a[:, 0:GROUP_WIDTH] for a in (sq0, sq1, sq2)], axis=-1).astype(F32)
    att_s, new_caches_t = _attn_sample(
        q_s, (sk0, sk1, sk2), [a[0] for a in (st0, st1, st2)], caches_t, n_s, t_new)
    u_ctx = jnp.concatenate([state_pool[0], us.reshape(n_s, t_new, POOL_WIDTH)], axis=1)
    new_pool_s = u_ctx[None, :, t_new:, :]
    to_ts = lambda a: a.reshape(n_s, t_new, -1).transpose(1, 0, 2).reshape(n_tok, -1)
    x2s = _merge_sample(to_ts(x1s), u_ctx.reshape(n_s, (POOL_STATE + t_new) * POOL_WIDTH),
                        to_ts(att_s), to_ts(gates), pw, ps, wbp, wba, wo, t_new)
    y_s = _ffn(x2s, g2, wgu2, wd2, tm=n_tok)
    y_sample = y_s.reshape(t_new, n_s, D_MODEL).transpose(1, 0, 2)
    new_kv_s = [_cols_to_cache(c) for c in new_caches_t]

    return (y_prompt, y_sample, new_kv_p[0], new_kv_p[1], new_kv_p[2], new_pool_p,
            new_kv_s[0], new_kv_s[1], new_kv_s[2], new_pool_s)
```

```python
import functools

import jax
import jax.numpy as jnp
from jax import lax
from jax.experimental import pallas as pl
from jax.experimental.pallas import tpu as pltpu

F32 = jnp.float32
BF16 = jnp.bfloat16

D_MODEL = 1024
D_FF = 2816
POOL_WINDOWS = (2, 4, 8, 16)
POOL_GROUP_WIDTH = 128
POOL_WIDTH = 512
POOL_STATE = 15
ATT_CONFIGS = ((128, 1), (512, 4), (2048, 16))
N_GROUPS = 3
HEADS = 8
HEAD_DIM = 64
GROUP_WIDTH = HEADS * HEAD_DIM
ATT_WIDTH = N_GROUPS * GROUP_WIDTH
ATT_BLK = 128
EPS = 1e-6
Q_OFF = POOL_WIDTH
K_OFF = Q_OFF + ATT_WIDTH
V_OFF = K_OFF + ATT_WIDTH
GATE_OFF = V_OFF + ATT_WIDTH
IN_WIDTH = GATE_OFF + 2 * D_MODEL

LANES = 128
FF_CHUNK = 256
VMEM_LIMIT = 56 * 1024 * 1024


def _params(n_axes, vmem=VMEM_LIMIT):
    return pltpu.CompilerParams(
        dimension_semantics=("arbitrary",) * n_axes, vmem_limit_bytes=vmem)


def _resident(shape):
    nd = len(shape)
    return pl.BlockSpec(shape, lambda *_: (0,) * nd, pipeline_mode=pl.Buffered(1))


def _rmsnorm(x, g):
    ms = jnp.mean(x * x, axis=-1, keepdims=True)
    return x * lax.rsqrt(ms + EPS) * g


def _mm(a, b):
    return jnp.dot(a, b, preferred_element_type=F32)


def _mm_nt(a, b):
    return lax.dot_general(a, b, (((1,), (1,)), ((), ())), preferred_element_type=F32)


def _ffn_kernel(x_ref, g_ref, wgu_ref, wd_ref, o_ref, acc_ref):
    x = x_ref[...]
    xn = _rmsnorm(x, g_ref[...]).astype(BF16)
    for c in range(D_FF // FF_CHUNK):
        lo = c * FF_CHUNK
        a = _mm(xn, wgu_ref[:, lo:lo + FF_CHUNK])
        b = _mm(xn, wgu_ref[:, D_FF + lo:D_FF + lo + FF_CHUNK])
        h = (a * jax.nn.sigmoid(a) * b).astype(BF16)
        d = _mm(h, wd_ref[lo:lo + FF_CHUNK, :])
        if c == 0:
            acc_ref[...] = d
        else:
            acc_ref[...] += d
    o_ref[...] = x + 0.5 * acc_ref[...]


def _ffn(x, g, wgu, wd, tm):
    t = x.shape[0]
    return pl.pallas_call(
        _ffn_kernel,
        grid=(t // tm,),
        in_specs=[
            pl.BlockSpec((tm, D_MODEL), lambda i: (i, 0)),
            _resident((1, D_MODEL)),
            _resident((D_MODEL, 2 * D_FF)),
            _resident((D_FF, D_MODEL)),
        ],
        out_specs=pl.BlockSpec((tm, D_MODEL), lambda i: (i, 0)),
        out_shape=jax.ShapeDtypeStruct((t, D_MODEL), F32),
        scratch_shapes=[pltpu.VMEM((tm, D_MODEL), F32)],
        compiler_params=_params(1),
        name="ffn",
    )(x, g, wgu, wd)


def _head_rmsnorm(x, g):
    low = lax.broadcasted_iota(jnp.int32, (1, LANES), 1) < HEAD_DIM
    outs = []
    for p in range(GROUP_WIDTH // LANES):
        xp = x[:, p * LANES:(p + 1) * LANES]
        sq = xp * xp
        s_all = jnp.sum(sq, axis=-1, keepdims=True)
        s_low = jnp.sum(jnp.where(low, sq, 0.0), axis=-1, keepdims=True)
        ms = jnp.where(low, s_low, s_all - s_low) * (1.0 / HEAD_DIM)
        outs.append(xp * lax.rsqrt(ms + EPS))
    return jnp.concatenate(outs, axis=-1) * g


def _in_kernel(x_ref, g_ref, w_ref, qn_ref, kn_ref, *refs, tm, dils, kv_from, kv_rows, row_kv):
    u_ref, gate_ref = refs[0:2]
    qkv_refs = refs[2:5]
    kvt_refs = refs[5:8]
    kvrow_refs = refs[8:11] if row_kv else None
    stage_ref, keep_ref = refs[-2:]
    j = pl.program_id(1)
    h = _rmsnorm(x_ref[...], g_ref[...]).astype(BF16)
    u_ref[...] = _mm(h, w_ref[:, 0:POOL_WIDTH])
    for g in range(N_GROUPS):
        c0, c1 = g * GROUP_WIDTH, (g + 1) * GROUP_WIDTH
        q = _head_rmsnorm(_mm(h, w_ref[:, Q_OFF + c0:Q_OFF + c1]), qn_ref[:, c0:c1])
        k = _head_rmsnorm(_mm(h, w_ref[:, K_OFF + c0:K_OFF + c1]), kn_ref[:, c0:c1])
        v = _mm(h, w_ref[:, V_OFF + c0:V_OFF + c1])
        d = dils[g]
        if d == 1:
            qkv_refs[g][:, 0:GROUP_WIDTH] = q.astype(BF16)
            qkv_refs[g][:, GROUP_WIDTH:2 * GROUP_WIDTH] = k.astype(BF16)
            qkv_refs[g][:, 2 * GROUP_WIDTH:3 * GROUP_WIDTH] = v.astype(BF16)
        else:
            for ci, a in enumerate((q, k, v)):
                for cl in range(GROUP_WIDTH // LANES):
                    stage_ref[ci * (GROUP_WIDTH // LANES) + cl] = a[:, cl * LANES:(cl + 1) * LANES]
            for r in range(d):
                for cl in range(3 * GROUP_WIDTH // LANES):
                    qkv_refs[g][0, r, :, cl * LANES:(cl + 1) * LANES] = (
                        stage_ref[cl, pl.ds(r, tm // d, stride=d), :].astype(BF16))
        if row_kv:
            kvrow_refs[g][:, 0:GROUP_WIDTH] = k
            kvrow_refs[g][:, GROUP_WIDTH:2 * GROUP_WIDTH] = v

        r0 = kv_rows[g]
        if kv_from[g] == 0:
            kvt_refs[g][0, 0:GROUP_WIDTH, :] = k[r0:, :].T
            kvt_refs[g][0, GROUP_WIDTH:2 * GROUP_WIDTH, :] = v[r0:, :].T
        else:
            keep_ref[g, 0, 0:tm - r0, :] = k[r0:, :]
            keep_ref[g, 1, 0:tm - r0, :] = v[r0:, :]
    gate_ref[...] = jax.nn.sigmoid(_mm(h, w_ref[:, GATE_OFF:IN_WIDTH])).astype(gate_ref.dtype)
    for g in range(N_GROUPS):
        if kv_from[g] > 0:
            @pl.when(j >= kv_from[g])
            def _(g=g):
                rows = tm - kv_rows[g]
                kvt_refs[g][0, 0:GROUP_WIDTH, :] = keep_ref[g, 0, 0:rows, :].T
                kvt_refs[g][0, GROUP_WIDTH:2 * GROUP_WIDTH, :] = keep_ref[g, 1, 0:rows, :].T


def _inproj(x1, g, w_in, qn, kn, n_seq, seq, tm, keeps, dils, row_kv):
    t = n_seq * seq
    nj = seq // tm
    row = lambda width: pl.BlockSpec((tm, width), lambda b, j: (b * nj + j, 0))
    qkv_specs, qkv_shapes = [], []
    for d in dils:
        if d == 1:
            qkv_specs.append(row(3 * GROUP_WIDTH))
            qkv_shapes.append(jax.ShapeDtypeStruct((t, 3 * GROUP_WIDTH), BF16))
        else:
            qkv_specs.append(pl.BlockSpec((1, d, tm // d, 3 * GROUP_WIDTH), lambda b, j: (b, 0, j, 0)))
            qkv_shapes.append(jax.ShapeDtypeStruct((n_seq, d, seq // d, 3 * GROUP_WIDTH), BF16))
    kv_from, kv_rows, kvt_specs, kvt_shapes = [], [], [], []
    for keep in keeps:
        tw = min(tm, keep)
        nb = keep // tw
        kv_from.append(nj - nb)
        kv_rows.append(tm - tw)
        kvt_specs.append(pl.BlockSpec(
            (1, 2 * GROUP_WIDTH, tw), lambda b, j, nb=nb: (b, 0, jnp.maximum(j - (nj - nb), 0))))
        kvt_shapes.append(jax.ShapeDtypeStruct((n_seq, 2 * GROUP_WIDTH, keep), F32))
    out_specs = [row(POOL_WIDTH), row(2 * D_MODEL)] + qkv_specs + kvt_specs
    out_shape = [jax.ShapeDtypeStruct((t, POOL_WIDTH), F32),
                 jax.ShapeDtypeStruct((t, 2 * D_MODEL), BF16)] + qkv_shapes + kvt_shapes
    if row_kv:
        out_specs += [row(2 * GROUP_WIDTH)] * N_GROUPS
        out_shape += [jax.ShapeDtypeStruct((t, 2 * GROUP_WIDTH), F32)] * N_GROUPS
    return pl.pallas_call(
        functools.partial(_in_kernel, tm=tm, dils=tuple(dils), kv_from=tuple(kv_from),
                          kv_rows=tuple(kv_rows), row_kv=row_kv),
        grid=(n_seq, nj),
        in_specs=[
            row(D_MODEL),
            _resident((1, D_MODEL)),
            _resident((D_MODEL, IN_WIDTH)),
            _resident((1, ATT_WIDTH)),
            _resident((1, ATT_WIDTH)),
        ],
        out_specs=out_specs,
        out_shape=out_shape,
        scratch_shapes=[pltpu.VMEM((3 * GROUP_WIDTH // LANES, tm, LANES), F32),
                        pltpu.VMEM((N_GROUPS, 2, tm, GROUP_WIDTH), F32)],
        compiler_params=_params(2),
        name="inproj",
    )(x1, g, w_in, qn, kn)


def _attn_kernel(qkv_ref, o_ref, l_ref, *, n_blocks):
    n_streams = qkv_ref.shape[1]
    i = lax.broadcasted_iota(jnp.int32, (ATT_BLK, 2 * ATT_BLK), 0)
    j = lax.broadcasted_iota(jnp.int32, (ATT_BLK, 2 * ATT_BLK), 1)
    band = (j >= i) & (j <= i + ATT_BLK)
    first = band[:, ATT_BLK:]
    low = lax.broadcasted_iota(jnp.int32, (ATT_BLK, LANES), 1) < HEAD_DIM
    scale = jnp.asarray(HEAD_DIM ** -0.5, BF16)

    def block(r, q0, k0, n_keys, mask):
        n_pairs = GROUP_WIDTH // LANES
        scores = []
        for p in range(n_pairs):
            c = p * LANES
            qp = qkv_ref[0, r, pl.ds(q0, ATT_BLK), c:c + LANES] * scale
            kp = qkv_ref[0, r, pl.ds(k0, n_keys), GROUP_WIDTH + c:GROUP_WIDTH + c + LANES]
            for own in (low, ~low):
                scores.append(_mm_nt(jnp.where(own, qp, jnp.zeros_like(qp)), kp))
        probs, inv_den, lse = [], [], []
        for s in scores:
            s = jnp.where(mask, s, -jnp.inf)
            mx = jnp.max(s, axis=-1, keepdims=True)
            e = jnp.exp(s - mx)
            den = jnp.sum(e, axis=-1, keepdims=True)
            probs.append(e.astype(BF16))
            inv_den.append(1.0 / den)
            lse.append(mx + jnp.log(den))
        for p in range(n_pairs):
            c = p * LANES
            vp = qkv_ref[0, r, pl.ds(k0, n_keys), 2 * GROUP_WIDTH + c:2 * GROUP_WIDTH + c + LANES]
            o_lo = _mm(probs[2 * p], vp) * inv_den[2 * p]
            o_hi = _mm(probs[2 * p + 1], vp) * inv_den[2 * p + 1]
            o_ref[0, r, pl.ds(q0, ATT_BLK), c:c + LANES] = jnp.where(low, o_lo, o_hi).astype(o_ref.dtype)
            l_ref[0, r, pl.ds(q0, ATT_BLK), c:c + LANES] = jnp.where(low, lse[2 * p], lse[2 * p + 1])

    def body(it, carry):
        if n_blocks == 1:
            block(it, 0, 0, ATT_BLK, first)
        else:
            r = it // n_blocks
            n = it % n_blocks

            @pl.when(n == 0)
            def _():
                block(r, 0, 0, ATT_BLK, first)

            @pl.when(n > 0)
            def _():
                q0 = pl.multiple_of(n * ATT_BLK, ATT_BLK)
                block(r, q0, pl.multiple_of(q0 - ATT_BLK, ATT_BLK), 2 * ATT_BLK, band)
        return carry

    lax.fori_loop(0, n_streams * n_blocks, body, 0)


def _attn_prompt(qkv, n_seq, seq, dil):
    ln = seq // dil
    out = jax.ShapeDtypeStruct((n_seq, dil, ln, GROUP_WIDTH), F32)
    ospec = pl.BlockSpec((1, dil, ln, GROUP_WIDTH), lambda b: (b, 0, 0, 0))
    return pl.pallas_call(
        functools.partial(_attn_kernel, n_blocks=ln // ATT_BLK),
        grid=(n_seq,),
        in_specs=[pl.BlockSpec((1, dil, ln, 3 * GROUP_WIDTH), lambda b: (b, 0, 0, 0))],
        out_specs=[ospec, ospec],
        out_shape=[jax.ShapeDtypeStruct(out.shape, BF16), out],
        compiler_params=_params(1),
        name=f"attn_d{dil}",
    )(qkv)


def _sattn_kernel(q_ref, kvn0_ref, kvn1_ref, kvn2_ref, kvt0_ref, kvt1_ref, kvt2_ref,
                  c0_ref, c1_ref, c2_ref, att_ref, n0_ref, n1_ref, n2_ref, *, t_new, seq_per_tile):
    b = pl.program_id(0)
    rows = t_new * HEADS
    r_t = lax.broadcasted_iota(jnp.int32, (rows, 1), 0) // HEADS
    r_h = lax.broadcasted_iota(jnp.int32, (rows, GROUP_WIDTH), 0) % HEADS
    diag = (lax.broadcasted_iota(jnp.int32, (rows, GROUP_WIDTH), 1) // HEAD_DIM) == r_h
    tail_lane = lax.broadcasted_iota(jnp.int32, (2 * GROUP_WIDTH, LANES), 1) >= LANES - t_new
    tail_shift = (LANES - t_new) - t_new * (b % seq_per_tile)
    groups = ((kvn0_ref, kvt0_ref, c0_ref, n0_ref), (kvn1_ref, kvt1_ref, c1_ref, n1_ref),
              (kvn2_ref, kvt2_ref, c2_ref, n2_ref))
    os_, ls = [], []
    for g, (kvn_ref, kvt_ref, c_ref, n_ref) in enumerate(groups):
        dil = ATT_CONFIGS[g][1]
        w = c_ref.shape[2]
        n_ref[0] = pltpu.roll(c_ref[0], w - t_new, axis=1)
        new_cols = pltpu.roll(kvt_ref[...], tail_shift, axis=1)
        n_ref[0, :, w - LANES:w] = jnp.where(tail_lane, new_cols, n_ref[0, :, w - LANES:w])
        kvn = kvn_ref[0]
        kn = kvn[:, 0:GROUP_WIDTH].astype(BF16).astype(F32)
        vn = kvn[:, GROUP_WIDTH:2 * GROUP_WIDTH].astype(BF16).astype(F32)
        qg = q_ref[0, :, g * GROUP_WIDTH:(g + 1) * GROUP_WIDTH].astype(BF16).astype(F32)
        qf = jnp.concatenate(
            [jnp.broadcast_to(qg[t:t + 1, :], (HEADS, GROUP_WIDTH)) for t in range(t_new)], axis=0)
        qf = jnp.where(diag, qf, 0.0)
        kc = c_ref[0, 0:GROUP_WIDTH, :].astype(BF16)
        vc = c_ref[0, GROUP_WIDTH:2 * GROUP_WIDTH, :].astype(BF16)
        wi = lax.broadcasted_iota(jnp.int32, (rows, w), 1)
        ok_c = (wi >= r_t) & (((wi - r_t) & (dil - 1)) == 0)
        s_c = jnp.where(ok_c, _mm(qf.astype(BF16), kc) * (HEAD_DIM ** -0.5), -jnp.inf)
        s_n = []
        for t2 in range(t_new):
            ok = (r_t >= t2) & (((r_t - t2) & (dil - 1)) == 0)
            s = jnp.sum(qf * kn[t2:t2 + 1, :], axis=-1, keepdims=True) * (HEAD_DIM ** -0.5)
            s_n.append(jnp.where(ok, s, -jnp.inf))
        mx = jnp.max(s_c, axis=-1, keepdims=True)
        for s in s_n:
            mx = jnp.maximum(mx, s)
        e_c = jnp.exp(s_c - mx)
        e_n = [jnp.exp(s - mx) for s in s_n]
        den = jnp.sum(e_c, axis=-1, keepdims=True)
        for e in e_n:
            den = den + e
        inv = 1.0 / den
        o = _mm_nt((e_c * inv).astype(BF16), vc)
        for t2 in range(t_new):
            o = o + (e_n[t2] * inv).astype(BF16).astype(F32) * vn[t2:t2 + 1, :]
        os_.append(o)
        ls.append(mx + jnp.log(den))
    full = jnp.where(diag, _combine_groups(os_, ls), 0.0)
    att_ref[0] = jnp.concatenate(
        [jnp.sum(full[t * HEADS:(t + 1) * HEADS, :], axis=0, keepdims=True) for t in range(t_new)],
        axis=0)


def _attn_sample(q, kvns, kvts, caches_t, n_seq, t_new):
    seq_per_tile = LANES // t_new
    q_v = q.reshape(n_seq, t_new, ATT_WIDTH)
    kvn_v = [a.reshape(n_seq, t_new, 2 * GROUP_WIDTH) for a in kvns]
    per_seq = lambda a: pl.BlockSpec((1,) + a.shape[1:], lambda b: (b, 0, 0))
    kvt_spec = pl.BlockSpec((2 * GROUP_WIDTH, LANES), lambda b: (0, b // seq_per_tile))
    att = jax.ShapeDtypeStruct((n_seq, t_new, GROUP_WIDTH), F32)
    outs = pl.pallas_call(
        functools.partial(_sattn_kernel, t_new=t_new, seq_per_tile=seq_per_tile),
        grid=(n_seq,),
        in_specs=[per_seq(a) for a in [q_v] + kvn_v] + [kvt_spec] * N_GROUPS
        + [per_seq(c) for c in caches_t],
        out_specs=[per_seq(att)] + [per_seq(c) for c in caches_t],
        out_shape=[att] + [jax.ShapeDtypeStruct(c.shape, F32) for c in caches_t],
        compiler_params=_params(1, vmem=60 * 1024 * 1024),
        name="attn_sample",
    )(q_v, *kvn_v, *kvts, *caches_t)
    return outs[0], outs[1:]


def _combine_groups(os_, ls):
    mx = jnp.maximum(jnp.maximum(ls[0], ls[1]), ls[2])
    es = [jnp.exp(l - mx) for l in ls]
    inv = 1.0 / (es[0] + es[1] + es[2])
    return (es[0] * inv) * os_[0] + (es[1] * inv) * os_[1] + (es[2] * inv) * os_[2]


def _merge_tail(x1, pooled, att, gate, pw_ref, ps_ref, wbp_ref, wba_ref, wo_ref):
    mixed = jnp.concatenate(
        [_mm(pooled[gi].astype(BF16), pw_ref[gi]) for gi in range(len(POOL_WINDOWS))], axis=-1)
    pool_y = (mixed * ps_ref[...]).astype(BF16)
    merged = (gate[:, 0:D_MODEL] * _mm(pool_y, wbp_ref[...])
              + gate[:, D_MODEL:2 * D_MODEL] * _mm(att.astype(BF16), wba_ref[...]))
    return x1 + _mm(merged.astype(BF16), wo_ref[...])


def _merge_prompt_kernel(x1_ref, u_ref, up_ref, o0_ref, o1_ref, o2_ref, l0_ref, l1_ref, l2_ref,
                         gate_ref, pw_ref, ps_ref, wbp_ref, wba_ref, wo_ref, y_ref,
                         ext_ref, nat_ref, *, tm):
    j = pl.program_id(1)
    u = u_ref[...]
    ext_ref[0:16, :] = jnp.where(j > 0, up_ref[...], 0.0)
    ext_ref[16:16 + tm, :] = u
    pos = j * tm + lax.broadcasted_iota(jnp.int32, (tm, 1), 0)
    pooled = []
    for gi, w in enumerate(POOL_WINDOWS):
        c0 = gi * POOL_GROUP_WIDTH
        acc = u[:, c0:c0 + POOL_GROUP_WIDTH]
        for k in range(1, w):
            acc = acc + ext_ref[16 - k:16 - k + tm, c0:c0 + POOL_GROUP_WIDTH]
        cnt = jnp.minimum(pos + 1, w).astype(F32)
        pooled.append(acc / cnt - u[:, c0:c0 + POOL_GROUP_WIDTH])

    def natural(ref, slot):
        if len(ref.shape) == 2:
            return ref[...].astype(F32)
        d = ref.shape[1]
        n_tiles = GROUP_WIDTH // LANES
        for r in range(d):
            for cl in range(n_tiles):
                nat_ref[slot * n_tiles + cl, pl.ds(r, tm // d, stride=d), :] = (
                    ref[0, r, :, cl * LANES:(cl + 1) * LANES].astype(F32))
        return jnp.concatenate([nat_ref[slot * n_tiles + cl] for cl in range(n_tiles)], axis=-1)

    att = _combine_groups((natural(o0_ref, 0), natural(o1_ref, 1), natural(o2_ref, 2)),
                          (natural(l0_ref, 3), natural(l1_ref, 4), natural(l2_ref, 5)))
    y_ref[...] = _merge_tail(x1_ref[...], pooled, att, gate_ref[...],
                             pw_ref, ps_ref, wbp_ref, wba_ref, wo_ref)


def _merge_prompt(x1, u, os_, ls, gate, pw, ps, wbp, wba, wo, n_seq, seq, tm):
    t = n_seq * seq
    nj = seq // tm
    row = lambda width: pl.BlockSpec((tm, width), lambda b, j: (b * nj + j, 0))
    prev = pl.BlockSpec((16, POOL_WIDTH),
                        lambda b, j: (jnp.maximum((b * nj + j) * (tm // 16) - 1, 0), 0))
    att_args, att_specs = [], []
    for a in list(os_) + list(ls):
        d = a.shape[1]
        if d == 1:
            att_args.append(a.reshape(t, GROUP_WIDTH))
            att_specs.append(row(GROUP_WIDTH))
        else:
            att_args.append(a)
            att_specs.append(pl.BlockSpec((1, d, tm // d, GROUP_WIDTH), lambda b, j: (b, 0, j, 0)))
    return pl.pallas_call(
        functools.partial(_merge_prompt_kernel, tm=tm),
        grid=(n_seq, nj),
        in_specs=[row(D_MODEL), row(POOL_WIDTH), prev] + att_specs + [row(2 * D_MODEL)]
        + [_resident(a.shape) for a in (pw, ps, wbp, wba, wo)],
        out_specs=row(D_MODEL),
        out_shape=jax.ShapeDtypeStruct((t, D_MODEL), F32),
        scratch_shapes=[pltpu.VMEM((tm + 16, POOL_WIDTH), F32),
                        pltpu.VMEM((2 * N_GROUPS * GROUP_WIDTH // LANES, tm, LANES), F32)],
        compiler_params=_params(2),
        name="merge_prompt",
    )(x1, u, u, *att_args, gate, pw, ps, wbp, wba, wo)


def _merge_sample_kernel(x1_ref, ctx_ref, att_ref, gate_ref, pw_ref, ps_ref, wbp_ref, wba_ref,
                         wo_ref, y_ref, *, t_new):
    n_seq = ctx_ref.shape[0]
    for t in range(t_new):
        def ctx_row(r, c0):
            return ctx_ref[:, r * POOL_WIDTH + c0:r * POOL_WIDTH + c0 + POOL_GROUP_WIDTH]
        pooled = []
        for gi, w in enumerate(POOL_WINDOWS):
            c0 = gi * POOL_GROUP_WIDTH
            last = POOL_STATE + t
            acc = ctx_row(last, c0)
            for k in range(1, w):
                acc = acc + ctx_row(last - k, c0)
            pooled.append(acc / float(min(w, last + 1)) - ctx_row(last, c0))
        rows = pl.ds(t * n_seq, n_seq)
        y_ref[rows, :] = _merge_tail(x1_ref[rows, :], pooled, att_ref[rows, :], gate_ref[rows, :],
                                     pw_ref, ps_ref, wbp_ref, wba_ref, wo_ref)


def _merge_sample(x1, ctx, att, gate, pw, ps, wbp, wba, wo, t_new):
    args = (x1, ctx, att, gate, pw, ps, wbp, wba, wo)
    return pl.pallas_call(
        functools.partial(_merge_sample_kernel, t_new=t_new),
        grid=(1,),
        in_specs=[pl.BlockSpec(a.shape, lambda i, nd=a.ndim: (0,) * nd) for a in args],
        out_specs=pl.BlockSpec(x1.shape, lambda i: (0, 0)),
        out_shape=jax.ShapeDtypeStruct(x1.shape, F32),
        compiler_params=_params(1),
        name="merge_sample",
    )(*args)


def _cache_to_cols(c):
    n_seq, w = c.shape[0], c.shape[1]
    return c.transpose(0, 2, 3, 4, 1).reshape(n_seq, 2 * GROUP_WIDTH, w)


def _cols_to_cache(ct):
    n_seq, _, w = ct.shape
    return ct.reshape(n_seq, 2, HEADS, HEAD_DIM, w).transpose(0, 4, 1, 2, 3)[None]


def kernel(x_prompt, x_sample, cache_kv_w128, cache_kv_w512, cache_kv_w2048, state_pool,
           ffn1_norm, ffn1_w_gu, ffn1_w_down, mix_norm, w_in, q_norm, k_norm, pool_w,
           pool_scale, w_branch_pool, w_branch_att, w_out, ffn2_norm, ffn2_w_gu, ffn2_w_down):
    depth = ffn1_norm.shape[0]
    assert depth == 1
    n_p, seq, _ = x_prompt.shape
    n_s, t_new, _ = x_sample.shape
    caches_t = [_cache_to_cols(c[0]) for c in (cache_kv_w128, cache_kv_w512, cache_kv_w2048)]
    dils = tuple(d for _, d in ATT_CONFIGS)

    g1 = ffn1_norm[0][None, :]
    g2 = ffn2_norm[0][None, :]
    gm = mix_norm[0][None, :]
    wgu1, wd1 = ffn1_w_gu[0].astype(BF16), ffn1_w_down[0].astype(BF16)
    wgu2, wd2 = ffn2_w_gu[0].astype(BF16), ffn2_w_down[0].astype(BF16)
    win = w_in[0].astype(BF16)
    qn = q_norm[0].reshape(1, ATT_WIDTH)
    kn = k_norm[0].reshape(1, ATT_WIDTH)
    pw = pool_w[0].astype(BF16)
    ps = pool_scale[0][None, :]
    wbp, wba, wo = (w_branch_pool[0].astype(BF16), w_branch_att[0].astype(BF16), w_out[0].astype(BF16))

    xp = x_prompt.reshape(n_p * seq, D_MODEL)
    x1 = _ffn(xp, g1, wgu1, wd1, tm=512)
    keeps = tuple(min(w, seq) for w, _ in ATT_CONFIGS)
    u, gate, qkv0, qkv1, qkv2, kvt0, kvt1, kvt2 = _inproj(
        x1, gm, win, qn, kn, n_p, seq, 256, keeps, dils, row_kv=False)
    os_, ls = [], []
    for qkv, dil in zip((qkv0, qkv1, qkv2), dils):
        o, l = _attn_prompt(qkv.reshape(n_p, dil, seq // dil, 3 * GROUP_WIDTH), n_p, seq, dil)
        os_.append(o)
        ls.append(l)
    x2 = _merge_prompt(x1, u, os_, ls, gate, pw, ps, wbp, wba, wo, n_p, seq, tm=256)
    y_prompt = _ffn(x2, g2, wgu2, wd2, tm=512).reshape(n_p, seq, D_MODEL)
    new_kv_p = [_cols_to_cache(kvt) for kvt in (kvt0, kvt1, kvt2)]
    new_pool_p = u.reshape(n_p, seq, POOL_WIDTH)[None, :, seq - POOL_STATE:, :]

    n_tok = n_s * t_new
    xs = x_sample.reshape(n_tok, D_MODEL)
    x1s = _ffn(xs, g1, wgu1, wd1, tm=n_tok)
    us, gates, sq0, sq1, sq2, st0, st1, st2, sk0, sk1, sk2 = _inproj(
        x1s, gm, win, qn, kn, 1, n_tok, 256, (n_tok,) * N_GROUPS, (1,) * N_GROUPS, row_kv=True)
    q_s = jnp.concatenate([a[:, 0:GROUP_WIDTH] for a in (sq0, sq1, sq2)], axis=-1).astype(F32)
    att_s, new_caches_t = _attn_sample(
        q_s, (sk0, sk1, sk2), [a[0] for a in (st0, st1, st2)], caches_t, n_s, t_new)
    u_ctx = jnp.concatenate([state_pool[0], us.reshape(n_s, t_new, POOL_WIDTH)], axis=1)
    new_pool_s = u_ctx[None, :, t_new:, :]
    to_ts = lambda a: a.reshape(n_s, t_new, -1).transpose(1, 0, 2).reshape(n_tok, -1)
    x2s = _merge_sample(to_ts(x1s), u_ctx.reshape(n_s, (POOL_STATE + t_new) * POOL_WIDTH),
                        to_ts(att_s), to_ts(gates), pw, ps, wbp, wba, wo, t_new)
    y_s = _ffn(x2s, g2, wgu2, wd2, tm=n_tok)
    y_sample = y_s.reshape(t_new, n_s, D_MODEL).transpose(1, 0, 2)
    new_kv_s = [_cols_to_cache(c) for c in new_caches_t]

    return (y_prompt, y_sample, new_kv_p[0], new_kv_p[1], new_kv_p[2], new_pool_p,
            new_kv_s[0], new_kv_s[1], new_kv_s[2], new_pool_s)
```

```python
import functools

import jax
import jax.numpy as jnp
from jax import lax
from jax.experimental import pallas as pl
from jax.experimental.pallas import tpu as pltpu

F32 = jnp.float32
BF16 = jnp.bfloat16

D_MODEL = 1024
D_FF = 2816
POOL_WINDOWS = (2, 4, 8, 16)
POOL_GROUP_WIDTH = 128
POOL_WIDTH = 512
POOL_STATE = 15
ATT_CONFIGS = ((128, 1), (512, 4), (2048, 16))
N_GROUPS = 3
HEADS = 8
HEAD_DIM = 64
GROUP_WIDTH = HEADS * HEAD_DIM
ATT_WIDTH = N_GROUPS * GROUP_WIDTH
ATT_BLK = 128
EPS = 1e-6
Q_OFF = POOL_WIDTH
K_OFF = Q_OFF + ATT_WIDTH
V_OFF = K_OFF + ATT_WIDTH
GATE_OFF = V_OFF + ATT_WIDTH
IN_WIDTH = GATE_OFF + 2 * D_MODEL

LANES = 128
FF_CHUNK = 256
VMEM_LIMIT = 56 * 1024 * 1024


def _params(n_axes, vmem=VMEM_LIMIT):
    return pltpu.CompilerParams(
        dimension_semantics=("arbitrary",) * n_axes, vmem_limit_bytes=vmem)


def _resident(shape):
    nd = len(shape)
    return pl.BlockSpec(shape, lambda *_: (0,) * nd, pipeline_mode=pl.Buffered(1))


def _rmsnorm(x, g):
    ms = jnp.mean(x * x, axis=-1, keepdims=True)
    return x * lax.rsqrt(ms + EPS) * g


def _mm(a, b):
    return jnp.dot(a, b, preferred_element_type=F32)


def _mm_nt(a, b):
    return lax.dot_general(a, b, (((1,), (1,)), ((), ())), preferred_element_type=F32)


def _roll_cache_half(c_ref, kvt_ref, n_ref, seq, t_new):
    w = c_ref.shape[2]
    tail_lane = lax.broadcasted_iota(jnp.int32, (GROUP_WIDTH, LANES), 1) >= LANES - t_new
    shift = (LANES - t_new) - t_new * (seq % (LANES // t_new))
    new_cols = pltpu.roll(kvt_ref[...], shift, axis=1)
    rolled = pltpu.roll(c_ref[0], w - t_new, axis=1)
    if w > LANES:
        n_ref[0, :, 0:w - LANES] = rolled[:, 0:w - LANES]
    n_ref[0, :, w - LANES:w] = jnp.where(tail_lane, new_cols, rolled[:, w - LANES:w])


def _ffn_kernel(x_ref, g_ref, wgu_ref, wd_ref, *refs, n_roll, t_new):
    o_ref, acc_ref = refs[-1 - n_roll - 1], refs[-1]
    for g in range(n_roll):
        _roll_cache_half(refs[g], refs[n_roll + g], refs[len(refs) - 1 - n_roll + g],
                         pl.program_id(0), t_new)
    x = x_ref[...]
    xn = _rmsnorm(x, g_ref[...]).astype(BF16)
    for c in range(D_FF // FF_CHUNK):
        lo = c * FF_CHUNK
        a = _mm(xn, wgu_ref[:, lo:lo + FF_CHUNK])
        b = _mm(xn, wgu_ref[:, D_FF + lo:D_FF + lo + FF_CHUNK])
        h = (a * jax.nn.sigmoid(a) * b).astype(BF16)
        d = _mm(h, wd_ref[lo:lo + FF_CHUNK, :])
        if c == 0:
            acc_ref[...] = d
        else:
            acc_ref[...] += d
    o_ref[...] = x + 0.5 * acc_ref[...]


def _ffn(x, g, wgu, wd, tm, roll=None):
    t = x.shape[0]
    steps = t // tm
    in_specs = [
        pl.BlockSpec((tm, D_MODEL), lambda i: (i, 0)),
        _resident((1, D_MODEL)),
        _resident((D_MODEL, 2 * D_FF)),
        _resident((D_FF, D_MODEL)),
    ]
    args = [x, g, wgu, wd]
    out_specs = [pl.BlockSpec((tm, D_MODEL), lambda i: (i, 0))]
    out_shape = [jax.ShapeDtypeStruct((t, D_MODEL), F32)]
    aliases = {}
    n_roll, t_new = 0, 0
    if roll is not None:
        caches_t, kvts, half, t_new, partial = roll
        n_roll = len(caches_t)
        assert all(c.shape[0] == steps for c in caches_t)
        per_tile = LANES // t_new
        half_specs = [pl.BlockSpec((1, GROUP_WIDTH, c.shape[2]), lambda i: (i, half, 0))
                      for c in caches_t]
        in_specs += half_specs
        in_specs += [pl.BlockSpec((GROUP_WIDTH, LANES), lambda i: (half, i // per_tile))] * n_roll
        args += list(caches_t) + list(kvts)
        if partial is not None:
            aliases = {len(args) + k: 1 + k for k in range(n_roll)}
            in_specs += [pl.BlockSpec(memory_space=pl.ANY)] * n_roll
            args += list(partial)
        out_specs += half_specs
        out_shape += [jax.ShapeDtypeStruct(c.shape, F32) for c in caches_t]
    outs = pl.pallas_call(
        functools.partial(_ffn_kernel, n_roll=n_roll, t_new=t_new),
        grid=(steps,),
        in_specs=in_specs,
        out_specs=out_specs,
        out_shape=out_shape,
        input_output_aliases=aliases,
        scratch_shapes=[pltpu.VMEM((tm, D_MODEL), F32)],
        compiler_params=_params(1),
        name="ffn",
    )(*args)
    return outs[0] if roll is None else (outs[0], outs[1:])


def _head_rmsnorm(x, g):
    low = lax.broadcasted_iota(jnp.int32, (1, LANES), 1) < HEAD_DIM
    outs = []
    for p in range(GROUP_WIDTH // LANES):
        xp = x[:, p * LANES:(p + 1) * LANES]
        sq = xp * xp
        s_all = jnp.sum(sq, axis=-1, keepdims=True)
        s_low = jnp.sum(jnp.where(low, sq, 0.0), axis=-1, keepdims=True)
        ms = jnp.where(low, s_low, s_all - s_low) * (1.0 / HEAD_DIM)
        outs.append(xp * lax.rsqrt(ms + EPS))
    return jnp.concatenate(outs, axis=-1) * g


def _in_kernel(x_ref, g_ref, w_ref, qn_ref, kn_ref, *refs, tm, dils, kv_from, kv_rows, row_kv):
    u_ref, gate_ref = refs[0:2]
    qkv_refs = refs[2:5]
    kvt_refs = refs[5:8]
    kvrow_refs = refs[8:11] if row_kv else None
    stage_ref, keep_ref = refs[-2:]
    j = pl.program_id(1)
    h = _rmsnorm(x_ref[...], g_ref[...]).astype(BF16)
    gate_ref[...] = jax.nn.sigmoid(_mm(h, w_ref[:, GATE_OFF:IN_WIDTH])).astype(gate_ref.dtype)
    for g in range(N_GROUPS):
        c0, c1 = g * GROUP_WIDTH, (g + 1) * GROUP_WIDTH
        q = _head_rmsnorm(_mm(h, w_ref[:, Q_OFF + c0:Q_OFF + c1]), qn_ref[:, c0:c1])
        k = _head_rmsnorm(_mm(h, w_ref[:, K_OFF + c0:K_OFF + c1]), kn_ref[:, c0:c1])
        v = _mm(h, w_ref[:, V_OFF + c0:V_OFF + c1])
        d = dils[g]
        if d == 1:
            qkv_refs[g][:, 0:GROUP_WIDTH] = q.astype(BF16)
            qkv_refs[g][:, GROUP_WIDTH:2 * GROUP_WIDTH] = k.astype(BF16)
            qkv_refs[g][:, 2 * GROUP_WIDTH:3 * GROUP_WIDTH] = v.astype(BF16)
        else:
            for ci, a in enumerate((q, k, v)):
                for cl in range(GROUP_WIDTH // LANES):
                    stage_ref[ci * (GROUP_WIDTH // LANES) + cl] = a[:, cl * LANES:(cl + 1) * LANES]
            for r in range(d):
                for cl in range(3 * GROUP_WIDTH // LANES):
                    qkv_refs[g][0, r, :, cl * LANES:(cl + 1) * LANES] = (
                        stage_ref[cl, pl.ds(r, tm // d, stride=d), :].astype(BF16))
        if row_kv:
            kvrow_refs[g][:, 0:GROUP_WIDTH] = k
            kvrow_refs[g][:, GROUP_WIDTH:2 * GROUP_WIDTH] = v

        r0 = kv_rows[g]
        if kv_from[g] == 0:
            kvt_refs[g][0, 0:GROUP_WIDTH, :] = k[r0:, :].T
            kvt_refs[g][0, GROUP_WIDTH:2 * GROUP_WIDTH, :] = v[r0:, :].T
        else:
            keep_ref[g, 0, 0:tm - r0, :] = k[r0:, :]
            keep_ref[g, 1, 0:tm - r0, :] = v[r0:, :]
    u_ref[...] = _mm(h, w_ref[:, 0:POOL_WIDTH])
    for g in range(N_GROUPS):
        if kv_from[g] > 0:
            @pl.when(j >= kv_from[g])
            def _(g=g):
                rows = tm - kv_rows[g]
                kvt_refs[g][0, 0:GROUP_WIDTH, :] = keep_ref[g, 0, 0:rows, :].T
                kvt_refs[g][0, GROUP_WIDTH:2 * GROUP_WIDTH, :] = keep_ref[g, 1, 0:rows, :].T


def _inproj(x1, g, w_in, qn, kn, n_seq, seq, tm, keeps, dils, row_kv):
    t = n_seq * seq
    nj = seq // tm
    row = lambda width: pl.BlockSpec((tm, width), lambda b, j: (b * nj + j, 0))
    qkv_specs, qkv_shapes = [], []
    for d in dils:
        if d == 1:
            qkv_specs.append(row(3 * GROUP_WIDTH))
            qkv_shapes.append(jax.ShapeDtypeStruct((t, 3 * GROUP_WIDTH), BF16))
        else:
            qkv_specs.append(pl.BlockSpec((1, d, tm // d, 3 * GROUP_WIDTH), lambda b, j: (b, 0, j, 0)))
            qkv_shapes.append(jax.ShapeDtypeStruct((n_seq, d, seq // d, 3 * GROUP_WIDTH), BF16))
    kv_from, kv_rows, kvt_specs, kvt_shapes = [], [], [], []
    for keep in keeps:
        tw = min(tm, keep)
        nb = keep // tw
        kv_from.append(nj - nb)
        kv_rows.append(tm - tw)
        kvt_specs.append(pl.BlockSpec(
            (1, 2 * GROUP_WIDTH, tw), lambda b, j, nb=nb: (b, 0, jnp.maximum(j - (nj - nb), 0))))
        kvt_shapes.append(jax.ShapeDtypeStruct((n_seq, 2 * GROUP_WIDTH, keep), F32))
    out_specs = [row(POOL_WIDTH), row(2 * D_MODEL)] + qkv_specs + kvt_specs
    out_shape = [jax.ShapeDtypeStruct((t, POOL_WIDTH), F32),
                 jax.ShapeDtypeStruct((t, 2 * D_MODEL), BF16)] + qkv_shapes + kvt_shapes
    if row_kv:
        out_specs += [row(2 * GROUP_WIDTH)] * N_GROUPS
        out_shape += [jax.ShapeDtypeStruct((t, 2 * GROUP_WIDTH), F32)] * N_GROUPS
    return pl.pallas_call(
        functools.partial(_in_kernel, tm=tm, dils=tuple(dils), kv_from=tuple(kv_from),
                          kv_rows=tuple(kv_rows), row_kv=row_kv),
        grid=(n_seq, nj),
        in_specs=[
            row(D_MODEL),
            _resident((1, D_MODEL)),
            _resident((D_MODEL, IN_WIDTH)),
            _resident((1, ATT_WIDTH)),
            _resident((1, ATT_WIDTH)),
        ],
        out_specs=out_specs,
        out_shape=out_shape,
        scratch_shapes=[pltpu.VMEM((3 * GROUP_WIDTH // LANES, tm, LANES), F32),
                        pltpu.VMEM((N_GROUPS, 2, tm, GROUP_WIDTH), F32)],
        compiler_params=_params(2),
        name="inproj",
    )(x1, g, w_in, qn, kn)


def _attn_kernel(qkv_ref, o_ref, l_ref, *, n_blocks):
    n_streams = qkv_ref.shape[1]
    i = lax.broadcasted_iota(jnp.int32, (ATT_BLK, 2 * ATT_BLK), 0)
    j = lax.broadcasted_iota(jnp.int32, (ATT_BLK, 2 * ATT_BLK), 1)
    band = (j >= i) & (j <= i + ATT_BLK)
    first = band[:, ATT_BLK:]
    low = lax.broadcasted_iota(jnp.int32, (ATT_BLK, LANES), 1) < HEAD_DIM
    scale = jnp.asarray(HEAD_DIM ** -0.5, BF16)

    def run(blocks):
        n_pairs = GROUP_WIDTH // LANES
        scores = []
        for r, q0, k0, n_keys, mask in blocks:
            for p in range(n_pairs):
                c = p * LANES
                qp = qkv_ref[0, r, pl.ds(q0, ATT_BLK), c:c + LANES] * scale
                kp = qkv_ref[0, r, pl.ds(k0, n_keys), GROUP_WIDTH + c:GROUP_WIDTH + c + LANES]
                for own in (low, ~low):
                    s = _mm_nt(jnp.where(own, qp, jnp.zeros_like(qp)), kp)
                    scores.append(jnp.where(mask, s, -jnp.inf))
        probs, inv_den, lse = [], [], []
        for s in scores:
            mx = jnp.max(s, axis=-1, keepdims=True)
            e = jnp.exp(s - mx)
            den = jnp.sum(e, axis=-1, keepdims=True)
            probs.append(e.astype(BF16))
            inv_den.append(1.0 / den)
            lse.append(mx + jnp.log(den))
        for bi, (r, q0, k0, n_keys, _) in enumerate(blocks):
            for p in range(n_pairs):
                c = p * LANES
                h0 = 2 * (bi * n_pairs + p)
                vp = qkv_ref[0, r, pl.ds(k0, n_keys), 2 * GROUP_WIDTH + c:2 * GROUP_WIDTH + c + LANES]
                o_lo = _mm(probs[h0], vp) * inv_den[h0]
                o_hi = _mm(probs[h0 + 1], vp) * inv_den[h0 + 1]
                o_ref[0, r, pl.ds(q0, ATT_BLK), c:c + LANES] = jnp.where(low, o_lo, o_hi).astype(o_ref.dtype)
                l_ref[0, r, pl.ds(q0, ATT_BLK), c:c + LANES] = jnp.where(low, lse[h0], lse[h0 + 1])

    def first_block(r):
        return (r, 0, 0, ATT_BLK, first)

    def band_block(r, n):
        q0, k0 = n * ATT_BLK, (n - 1) * ATT_BLK
        if not isinstance(n, int):
            q0, k0 = pl.multiple_of(q0, ATT_BLK), pl.multiple_of(k0, ATT_BLK)
        return (r, q0, k0, 2 * ATT_BLK, band)

    if n_blocks == 1:
        def body(it, carry):
            run([first_block(2 * it), first_block(2 * it + 1)])
            return carry
        lax.fori_loop(0, n_streams // 2, body, 0)
    else:
        half = n_blocks // 2

        def body(it, carry):
            r = it // half
            m = it % half

            @pl.when(m == 0)
            def _():
                run([first_block(r), band_block(r, 1)])

            @pl.when(m > 0)
            def _():
                run([band_block(r, 2 * m), band_block(r, 2 * m + 1)])
            return carry
        lax.fori_loop(0, n_streams * half, body, 0)


def _attn_prompt(qkv, n_seq, seq, dil):
    ln = seq // dil
    out = jax.ShapeDtypeStruct((n_seq, dil, ln, GROUP_WIDTH), F32)
    ospec = pl.BlockSpec((1, dil, ln, GROUP_WIDTH), lambda b: (b, 0, 0, 0))
    return pl.pallas_call(
        functools.partial(_attn_kernel, n_blocks=ln // ATT_BLK),
        grid=(n_seq,),
        in_specs=[pl.BlockSpec((1, dil, ln, 3 * GROUP_WIDTH), lambda b: (b, 0, 0, 0))],
        out_specs=[ospec, ospec],
        out_shape=[jax.ShapeDtypeStruct(out.shape, BF16), out],
        compiler_params=_params(1),
        name=f"attn_d{dil}",
    )(qkv)


def _sattn_kernel(q_ref, kvn0_ref, kvn1_ref, kvn2_ref, c0_ref, c1_ref, c2_ref, att_ref, *, t_new):
    rows = t_new * HEADS
    r_t = lax.broadcasted_iota(jnp.int32, (rows, 1), 0) // HEADS
    r_h = lax.broadcasted_iota(jnp.int32, (rows, GROUP_WIDTH), 0) % HEADS
    diag = (lax.broadcasted_iota(jnp.int32, (rows, GROUP_WIDTH), 1) // HEAD_DIM) == r_h
    groups = ((kvn0_ref, c0_ref), (kvn1_ref, c1_ref), (kvn2_ref, c2_ref))
    os_, ls = [], []
    for g, (kvn_ref, c_ref) in enumerate(groups):
        dil = ATT_CONFIGS[g][1]
        w = c_ref.shape[2]
        kvn = kvn_ref[0]
        kn = kvn[:, 0:GROUP_WIDTH].astype(BF16).astype(F32)
        vn = kvn[:, GROUP_WIDTH:2 * GROUP_WIDTH].astype(BF16).astype(F32)
        qg = q_ref[0, :, g * GROUP_WIDTH:(g + 1) * GROUP_WIDTH].astype(BF16).astype(F32)
        qf = jnp.concatenate(
            [jnp.broadcast_to(qg[t:t + 1, :], (HEADS, GROUP_WIDTH)) for t in range(t_new)], axis=0)
        qf = jnp.where(diag, qf, 0.0)
        kc = c_ref[0, 0:GROUP_WIDTH, :].astype(BF16)
        vc = c_ref[0, GROUP_WIDTH:2 * GROUP_WIDTH, :].astype(BF16)
        wi = lax.broadcasted_iota(jnp.int32, (rows, w), 1)
        ok_c = (wi >= r_t) & (((wi - r_t) & (dil - 1)) == 0)
        s_c = jnp.where(ok_c, _mm(qf.astype(BF16), kc) * (HEAD_DIM ** -0.5), -jnp.inf)
        s_n = []
        for t2 in range(t_new):
            ok = (r_t >= t2) & (((r_t - t2) & (dil - 1)) == 0)
            s = jnp.sum(qf * kn[t2:t2 + 1, :], axis=-1, keepdims=True) * (HEAD_DIM ** -0.5)
            s_n.append(jnp.where(ok, s, -jnp.inf))
        mx = jnp.max(s_c, axis=-1, keepdims=True)
        for s in s_n:
            mx = jnp.maximum(mx, s)
        e_c = jnp.exp(s_c - mx)
        e_n = [jnp.exp(s - mx) for s in s_n]
        den = jnp.sum(e_c, axis=-1, keepdims=True)
        for e in e_n:
            den = den + e
        inv = 1.0 / den
        o = _mm_nt((e_c * inv).astype(BF16), vc)
        for t2 in range(t_new):
            o = o + (e_n[t2] * inv).astype(BF16).astype(F32) * vn[t2:t2 + 1, :]
        os_.append(o)
        ls.append(mx + jnp.log(den))
    full = jnp.where(diag, _combine_groups(os_, ls), 0.0)
    att_ref[0] = jnp.concatenate(
        [jnp.sum(full[t * HEADS:(t + 1) * HEADS, :], axis=0, keepdims=True) for t in range(t_new)],
        axis=0)


def _attn_sample(q, kvns, caches_t, n_seq, t_new):
    q_v = q.reshape(n_seq, t_new, ATT_WIDTH)
    kvn_v = [a.reshape(n_seq, t_new, 2 * GROUP_WIDTH) for a in kvns]
    per_seq = lambda a: pl.BlockSpec((1,) + a.shape[1:], lambda b: (b, 0, 0))
    att = jax.ShapeDtypeStruct((n_seq, t_new, GROUP_WIDTH), F32)
    return pl.pallas_call(
        functools.partial(_sattn_kernel, t_new=t_new),
        grid=(n_seq,),
        in_specs=[per_seq(a) for a in [q_v] + kvn_v + list(caches_t)],
        out_specs=per_seq(att),
        out_shape=att,
        compiler_params=_params(1),
        name="attn_sample",
    )(q_v, *kvn_v, *caches_t)


def _combine_groups(os_, ls):
    mx = jnp.maximum(jnp.maximum(ls[0], ls[1]), ls[2])
    es = [jnp.exp(l - mx) for l in ls]
    inv = 1.0 / (es[0] + es[1] + es[2])
    return (es[0] * inv) * os_[0] + (es[1] * inv) * os_[1] + (es[2] * inv) * os_[2]


def _merge_tail(x1, pooled, att, gate, pw_ref, ps_ref, wbp_ref, wba_ref, wo_ref):
    mixed = jnp.concatenate(
        [_mm(pooled[gi].astype(BF16), pw_ref[gi]) for gi in range(len(POOL_WINDOWS))], axis=-1)
    pool_y = (mixed * ps_ref[...]).astype(BF16)
    merged = (gate[:, 0:D_MODEL] * _mm(pool_y, wbp_ref[...])
              + gate[:, D_MODEL:2 * D_MODEL] * _mm(att.astype(BF16), wba_ref[...]))
    return x1 + _mm(merged.astype(BF16), wo_ref[...])


def _merge_prompt_kernel(x1_ref, u_ref, up_ref, o0_ref, o1_ref, o2_ref, l0_ref, l1_ref, l2_ref,
                         gate_ref, pw_ref, ps_ref, wbp_ref, wba_ref, wo_ref, y_ref,
                         ext_ref, nat_ref, *, tm):
    j = pl.program_id(1)
    u = u_ref[...]
    ext_ref[0:16, :] = jnp.where(j > 0, up_ref[...], 0.0)
    ext_ref[16:16 + tm, :] = u
    pos = j * tm + lax.broadcasted_iota(jnp.int32, (tm, 1), 0)
    pooled = []
    for gi, w in enumerate(POOL_WINDOWS):
        c0 = gi * POOL_GROUP_WIDTH
        acc = u[:, c0:c0 + POOL_GROUP_WIDTH]
        for k in range(1, w):
            acc = acc + ext_ref[16 - k:16 - k + tm, c0:c0 + POOL_GROUP_WIDTH]
        cnt = jnp.minimum(pos + 1, w).astype(F32)
        pooled.append(acc / cnt - u[:, c0:c0 + POOL_GROUP_WIDTH])

    def natural(ref, slot):
        if len(ref.shape) == 2:
            return ref[...].astype(F32)
        d = ref.shape[1]
        n_tiles = GROUP_WIDTH // LANES
        for r in range(d):
            for cl in range(n_tiles):
                nat_ref[slot * n_tiles + cl, pl.ds(r, tm // d, stride=d), :] = (
                    ref[0, r, :, cl * LANES:(cl + 1) * LANES].astype(F32))
        return jnp.concatenate([nat_ref[slot * n_tiles + cl] for cl in range(n_tiles)], axis=-1)

    att = _combine_groups((natural(o0_ref, 0), natural(o1_ref, 1), natural(o2_ref, 2)),
                          (natural(l0_ref, 3), natural(l1_ref, 4), natural(l2_ref, 5)))
    y_ref[...] = _merge_tail(x1_ref[...], pooled, att, gate_ref[...],
                             pw_ref, ps_ref, wbp_ref, wba_ref, wo_ref)


def _merge_prompt(x1, u, os_, ls, gate, pw, ps, wbp, wba, wo, n_seq, seq, tm):
    t = n_seq * seq
    nj = seq // tm
    row = lambda width: pl.BlockSpec((tm, width), lambda b, j: (b * nj + j, 0))
    prev = pl.BlockSpec((16, POOL_WIDTH),
                        lambda b, j: (jnp.maximum((b * nj + j) * (tm // 16) - 1, 0), 0))
    att_args, att_specs = [], []
    for a in list(os_) + list(ls):
        d = a.shape[1]
        if d == 1:
            att_args.append(a.reshape(t, GROUP_WIDTH))
            att_specs.append(row(GROUP_WIDTH))
        else:
            att_args.append(a)
            att_specs.append(pl.BlockSpec((1, d, tm // d, GROUP_WIDTH), lambda b, j: (b, 0, j, 0)))
    return pl.pallas_call(
        functools.partial(_merge_prompt_kernel, tm=tm),
        grid=(n_seq, nj),
        in_specs=[row(D_MODEL), row(POOL_WIDTH), prev] + att_specs + [row(2 * D_MODEL)]
        + [_resident(a.shape) for a in (pw, ps, wbp, wba, wo)],
        out_specs=row(D_MODEL),
        out_shape=jax.ShapeDtypeStruct((t, D_MODEL), F32),
        scratch_shapes=[pltpu.VMEM((tm + 16, POOL_WIDTH), F32),
                        pltpu.VMEM((2 * N_GROUPS * GROUP_WIDTH // LANES, tm, LANES), F32)],
        compiler_params=_params(2),
        name="merge_prompt",
    )(x1, u, u, *att_args, gate, pw, ps, wbp, wba, wo)


def _merge_sample_kernel(x1_ref, ctx_ref, att_ref, gate_ref, pw_ref, ps_ref, wbp_ref, wba_ref,
                         wo_ref, y_ref, *, t_new):
    n_seq = ctx_ref.shape[0]
    for t in range(t_new):
        def ctx_row(r, c0):
            return ctx_ref[:, r * POOL_WIDTH + c0:r * POOL_WIDTH + c0 + POOL_GROUP_WIDTH]
        pooled = []
        for gi, w in enumerate(POOL_WINDOWS):
            c0 = gi * POOL_GROUP_WIDTH
            last = POOL_STATE + t
            acc = ctx_row(last, c0)
            for k in range(1, w):
                acc = acc + ctx_row(last - k, c0)
            pooled.append(acc / float(min(w, last + 1)) - ctx_row(last, c0))
        rows = pl.ds(t * n_seq, n_seq)
        y_ref[rows, :] = _merge_tail(x1_ref[rows, :], pooled, att_ref[rows, :], gate_ref[rows, :],
                                     pw_ref, ps_ref, wbp_ref, wba_ref, wo_ref)


def _merge_sample(x1, ctx, att, gate, pw, ps, wbp, wba, wo, t_new):
    args = (x1, ctx, att, gate, pw, ps, wbp, wba, wo)
    return pl.pallas_call(
        functools.partial(_merge_sample_kernel, t_new=t_new),
        grid=(1,),
        in_specs=[pl.BlockSpec(a.shape, lambda i, nd=a.ndim: (0,) * nd) for a in args],
        out_specs=pl.BlockSpec(x1.shape, lambda i: (0, 0)),
        out_shape=jax.ShapeDtypeStruct(x1.shape, F32),
        compiler_params=_params(1),
        name="merge_sample",
    )(*args)


def _cache_to_cols(c):
    n_seq, w = c.shape[0], c.shape[1]
    return c.transpose(0, 2, 3, 4, 1).reshape(n_seq, 2 * GROUP_WIDTH, w)


def _cols_to_cache(ct):
    n_seq, _, w = ct.shape
    return ct.reshape(n_seq, 2, HEADS, HEAD_DIM, w).transpose(0, 4, 1, 2, 3)[None]


def kernel(x_prompt, x_sample, cache_kv_w128, cache_kv_w512, cache_kv_w2048, state_pool,
           ffn1_norm, ffn1_w_gu, ffn1_w_down, mix_norm, w_in, q_norm, k_norm, pool_w,
           pool_scale, w_branch_pool, w_branch_att, w_out, ffn2_norm, ffn2_w_gu, ffn2_w_down):
    depth = ffn1_norm.shape[0]
    assert depth == 1
    n_p, seq, _ = x_prompt.shape
    n_s, t_new, _ = x_sample.shape
    caches_t = [_cache_to_cols(c[0]) for c in (cache_kv_w128, cache_kv_w512, cache_kv_w2048)]
    dils = tuple(d for _, d in ATT_CONFIGS)

    g1 = ffn1_norm[0][None, :]
    g2 = ffn2_norm[0][None, :]
    gm = mix_norm[0][None, :]
    wgu1, wd1 = ffn1_w_gu[0].astype(BF16), ffn1_w_down[0].astype(BF16)
    wgu2, wd2 = ffn2_w_gu[0].astype(BF16), ffn2_w_down[0].astype(BF16)
    win = w_in[0].astype(BF16)
    qn = q_norm[0].reshape(1, ATT_WIDTH)
    kn = k_norm[0].reshape(1, ATT_WIDTH)
    pw = pool_w[0].astype(BF16)
    ps = pool_scale[0][None, :]
    wbp, wba, wo = (w_branch_pool[0].astype(BF16), w_branch_att[0].astype(BF16), w_out[0].astype(BF16))

    n_tok = n_s * t_new
    xs = x_sample.reshape(n_tok, D_MODEL)
    x1s = _ffn(xs, g1, wgu1, wd1, tm=n_tok)
    us, gates, sq0, sq1, sq2, st0, st1, st2, sk0, sk1, sk2 = _inproj(
        x1s, gm, win, qn, kn, 1, n_tok, 256, (n_tok,) * N_GROUPS, (1,) * N_GROUPS, row_kv=True)
    new_cols = [a[0] for a in (st0, st1, st2)]
    q_s = jnp.concatenate([a[:, 0:GROUP_WIDTH] for a in (sq0, sq1, sq2)], axis=-1).astype(F32)
    att_s = _attn_sample(q_s, (sk0, sk1, sk2), caches_t, n_s, t_new)

    xp = x_prompt.reshape(n_p * seq, D_MODEL)
    tm_ffn = n_p * seq // n_s
    x1, half_rolled = _ffn(xp, g1, wgu1, wd1, tm_ffn, roll=(caches_t, new_cols, 0, t_new, None))
    keeps = tuple(min(w, seq) for w, _ in ATT_CONFIGS)
    u, gate, qkv0, qkv1, qkv2, kvt0, kvt1, kvt2 = _inproj(
        x1, gm, win, qn, kn, n_p, seq, 256, keeps, dils, row_kv=False)
    os_, ls = [], []
    for qkv, dil in zip((qkv0, qkv1, qkv2), dils):
        o, l = _attn_prompt(qkv.reshape(n_p, dil, seq // dil, 3 * GROUP_WIDTH), n_p, seq, dil)
        os_.append(o)
        ls.append(l)
    x2 = _merge_prompt(x1, u, os_, ls, gate, pw, ps, wbp, wba, wo, n_p, seq, tm=256)
    y_p, new_caches_t = _ffn(x2, g2, wgu2, wd2, tm_ffn,
                             roll=(caches_t, new_cols, 1, t_new, half_rolled))
    y_prompt = y_p.reshape(n_p, seq, D_MODEL)
    new_kv_p = [_cols_to_cache(kvt) for kvt in (kvt0, kvt1, kvt2)]
    new_pool_p = u.reshape(n_p, seq, POOL_WIDTH)[None, :, seq - POOL_STATE:, :]
    new_kv_s = [_cols_to_cache(c) for c in new_caches_t]

    u_ctx = jnp.concatenate([state_pool[0], us.reshape(n_s, t_new, POOL_WIDTH)], axis=1)
    new_pool_s = u_ctx[None, :, t_new:, :]
    to_ts = lambda a: a.reshape(n_s, t_new, -1).transpose(1, 0, 2).reshape(n_tok, -1)
    x2s = _merge_sample(to_ts(x1s), u_ctx.reshape(n_s, (POOL_STATE + t_new) * POOL_WIDTH),
                        to_ts(att_s), to_ts(gates), pw, ps, wbp, wba, wo, t_new)
    y_s = _ffn(x2s, g2, wgu2, wd2, tm=n_tok)
    y_sample = y_s.reshape(t_new, n_s, D_MODEL).transpose(1, 0, 2)

    return (y_prompt, y_sample, new_kv_p[0], new_kv_p[1], new_kv_p[2], new_pool_p,
            new_kv_s[0], new_kv_s[1], new_kv_s[2], new_pool_s)
```

```python
import functools

import jax
import jax.numpy as jnp
from jax import lax
from jax.experimental import pallas as pl
from jax.experimental.pallas import tpu as pltpu

F32 = jnp.float32
BF16 = jnp.bfloat16

D_MODEL = 1024
D_FF = 2816
POOL_WINDOWS = (2, 4, 8, 16)
POOL_GROUP_WIDTH = 128
POOL_WIDTH = 512
POOL_STATE = 15
ATT_CONFIGS = ((128, 1), (512, 4), (2048, 16))
N_GROUPS = 3
HEADS = 8
HEAD_DIM = 64
GROUP_WIDTH = HEADS * HEAD_DIM
ATT_WIDTH = N_GROUPS * GROUP_WIDTH
ATT_BLK = 128
EPS = 1e-6
Q_OFF = POOL_WIDTH
K_OFF = Q_OFF + ATT_WIDTH
V_OFF = K_OFF + ATT_WIDTH
GATE_OFF = V_OFF + ATT_WIDTH
IN_WIDTH = GATE_OFF + 2 * D_MODEL

LANES = 128
FF_CHUNK = 256
VMEM_LIMIT = 56 * 1024 * 1024
VMEM_LIMIT_FULL = 62 * 1024 * 1024


def _params(n_axes, vmem=VMEM_LIMIT):
    return pltpu.CompilerParams(
        dimension_semantics=("arbitrary",) * n_axes, vmem_limit_bytes=vmem)


def _resident(shape):
    nd = len(shape)
    return pl.BlockSpec(shape, lambda *_: (0,) * nd, pipeline_mode=pl.Buffered(1))


def _rmsnorm(x, g):
    ms = jnp.mean(x * x, axis=-1, keepdims=True)
    return x * lax.rsqrt(ms + EPS) * g


def _mm(a, b):
    return jnp.dot(a, b, preferred_element_type=F32)


def _mm_nt(a, b):
    return lax.dot_general(a, b, (((1,), (1,)), ((), ())), preferred_element_type=F32)


def _roll_cache_half(c_ref, kvt_ref, n_ref, seq, t_new):
    w = c_ref.shape[2]
    tail_lane = lax.broadcasted_iota(jnp.int32, (GROUP_WIDTH, LANES), 1) >= LANES - t_new
    shift = (LANES - t_new) - t_new * (seq % (LANES // t_new))
    new_cols = pltpu.roll(kvt_ref[...], shift, axis=1)
    rolled = pltpu.roll(c_ref[0], w - t_new, axis=1)
    if w > LANES:
        n_ref[0, :, 0:w - LANES] = rolled[:, 0:w - LANES]
    n_ref[0, :, w - LANES:w] = jnp.where(tail_lane, new_cols, rolled[:, w - LANES:w])


def _sample_rows(t_new):
    rows = t_new * HEADS
    r_t = lax.broadcasted_iota(jnp.int32, (rows, 1), 0) // HEADS
    r_h = lax.broadcasted_iota(jnp.int32, (rows, GROUP_WIDTH), 0) % HEADS
    diag = (lax.broadcasted_iota(jnp.int32, (rows, GROUP_WIDTH), 1) // HEAD_DIM) == r_h
    return rows, r_t, diag


def _sample_scores(q_ref, kvn_refs, k_refs, p_refs, st_refs, t_new):
    rows, r_t, diag = _sample_rows(t_new)
    lane = lax.broadcasted_iota(jnp.int32, (rows, LANES), 1)
    for g in range(N_GROUPS):
        dil = ATT_CONFIGS[g][1]
        w = k_refs[g].shape[2]
        kn = kvn_refs[g][0][:, 0:GROUP_WIDTH].astype(BF16).astype(F32)
        qg = q_ref[0, :, g * GROUP_WIDTH:(g + 1) * GROUP_WIDTH].astype(BF16).astype(F32)
        qf = jnp.concatenate(
            [jnp.broadcast_to(qg[t:t + 1, :], (HEADS, GROUP_WIDTH)) for t in range(t_new)], axis=0)
        qf = jnp.where(diag, qf, 0.0)
        wi = lax.broadcasted_iota(jnp.int32, (rows, w), 1)
        ok_c = (wi >= r_t) & (((wi - r_t) & (dil - 1)) == 0)
        s_c = _mm(qf.astype(BF16), k_refs[g][0].astype(BF16)) * (HEAD_DIM ** -0.5)
        s_c = jnp.where(ok_c, s_c, -jnp.inf)
        s_n = []
        for t2 in range(t_new):
            ok = (r_t >= t2) & (((r_t - t2) & (dil - 1)) == 0)
            s = jnp.sum(qf * kn[t2:t2 + 1, :], axis=-1, keepdims=True) * (HEAD_DIM ** -0.5)
            s_n.append(jnp.where(ok, s, -jnp.inf))
        mx = jnp.max(s_c, axis=-1, keepdims=True)
        for s in s_n:
            mx = jnp.maximum(mx, s)
        e_c = jnp.exp(s_c - mx)
        e_n = [jnp.exp(s - mx) for s in s_n]
        den = jnp.sum(e_c, axis=-1, keepdims=True)
        for e in e_n:
            den = den + e
        inv = 1.0 / den
        p_refs[g][0] = (e_c * inv).astype(BF16)
        st = jnp.where(lane == t_new, mx + jnp.log(den), 0.0)
        for t2 in range(t_new):
            st = jnp.where(lane == t2, e_n[t2] * inv, st)
        st_refs[g][0] = st


def _sample_values(p_refs, st_refs, kvn_refs, v_refs, att_ref, t_new):
    rows, _, diag = _sample_rows(t_new)
    os_, ls = [], []
    for g in range(N_GROUPS):
        st = st_refs[g][0]
        vn = kvn_refs[g][0][:, GROUP_WIDTH:2 * GROUP_WIDTH].astype(BF16).astype(F32)
        o = _mm_nt(p_refs[g][0], v_refs[g][0].astype(BF16))
        for t2 in range(t_new):
            o = o + st[:, t2:t2 + 1].astype(BF16).astype(F32) * vn[t2:t2 + 1, :]
        os_.append(o)
        ls.append(st[:, t_new:t_new + 1])
    full = jnp.where(diag, _combine_groups(os_, ls), 0.0)
    att_ref[0] = jnp.concatenate(
        [jnp.sum(full[t * HEADS:(t + 1) * HEADS, :], axis=0, keepdims=True) for t in range(t_new)],
        axis=0)


def _ffn_kernel(x_ref, g_ref, wgu_ref, wd_ref, *refs, n_roll, aliased, t_new, stage):
    it = iter(refs)
    take = lambda n: [next(it) for _ in range(n)]
    c_refs, kvt_refs = take(n_roll), take(n_roll)
    take(n_roll if aliased else 0)
    if stage == "scores":
        q_ref, kvn_refs = next(it), take(n_roll)
    elif stage == "values":
        p_refs, st_refs, kvn_refs = take(n_roll), take(n_roll), take(n_roll)
    o_ref = next(it)
    n_refs = take(n_roll)
    if stage == "scores":
        _sample_scores(q_ref, kvn_refs, c_refs, take(n_roll), take(n_roll), t_new)
    elif stage == "values":
        _sample_values(p_refs, st_refs, kvn_refs, c_refs, next(it), t_new)
    acc_ref = next(it)
    for g in range(n_roll):
        _roll_cache_half(c_refs[g], kvt_refs[g], n_refs[g], pl.program_id(0), t_new)
    x = x_ref[...]
    xn = _rmsnorm(x, g_ref[...]).astype(BF16)
    for c, lo in enumerate(range(0, D_FF, FF_CHUNK)):
        hi = min(lo + FF_CHUNK, D_FF)
        a = _mm(xn, wgu_ref[:, lo:hi])
        b = _mm(xn, wgu_ref[:, D_FF + lo:D_FF + hi])
        h = (a * jax.nn.sigmoid(a) * b).astype(BF16)
        d = _mm(h, wd_ref[lo:hi, :])
        if c == 0:
            acc_ref[...] = d
        else:
            acc_ref[...] += d
    o_ref[...] = x + 0.5 * acc_ref[...]


def _ffn(x, g, wgu, wd, tm, roll=None, stage_args=()):
    t = x.shape[0]
    steps = t // tm
    in_specs = [
        pl.BlockSpec((tm, D_MODEL), lambda i: (i, 0)),
        _resident((1, D_MODEL)),
        _resident((D_MODEL, 2 * D_FF)),
        _resident((D_FF, D_MODEL)),
    ]
    args = [x, g, wgu, wd]
    out_specs = [pl.BlockSpec((tm, D_MODEL), lambda i: (i, 0))]
    out_shape = [jax.ShapeDtypeStruct((t, D_MODEL), F32)]
    aliases = {}
    n_roll, t_new, stage = 0, 0, None
    if roll is not None:
        caches_t, kvts, half, t_new, partial = roll
        n_roll = len(caches_t)
        assert all(c.shape[0] == steps for c in caches_t)
        per_tile = LANES // t_new
        half_specs = [pl.BlockSpec((1, GROUP_WIDTH, c.shape[2]), lambda i: (i, half, 0))
                      for c in caches_t]
        in_specs += half_specs
        in_specs += [pl.BlockSpec((GROUP_WIDTH, LANES), lambda i: (half, i // per_tile))] * n_roll
        args += list(caches_t) + list(kvts)
        if partial is not None:
            aliases = {len(args) + k: 1 + k for k in range(n_roll)}
            in_specs += [pl.BlockSpec(memory_space=pl.ANY)] * n_roll
            args += list(partial)
        out_specs += half_specs
        out_shape += [jax.ShapeDtypeStruct(c.shape, F32) for c in caches_t]
        stage = ("scores", "values")[half]
        per_seq = lambda a: pl.BlockSpec((1,) + a.shape[1:], lambda i: (i, 0, 0))
        in_specs += [per_seq(a) for a in stage_args]
        args += list(stage_args)
        n_seq, rows = steps, t_new * HEADS
        if stage == "scores":
            stage_out = ([jax.ShapeDtypeStruct((n_seq, rows, c.shape[2]), BF16) for c in caches_t]
                         + [jax.ShapeDtypeStruct((n_seq, rows, LANES), F32)] * n_roll)
        else:
            stage_out = [jax.ShapeDtypeStruct((n_seq, t_new, GROUP_WIDTH), F32)]
        out_specs += [per_seq(a) for a in stage_out]
        out_shape += stage_out
    outs = pl.pallas_call(
        functools.partial(_ffn_kernel, n_roll=n_roll, aliased=bool(aliases), t_new=t_new,
                          stage=stage),
        grid=(steps,),
        in_specs=in_specs,
        out_specs=out_specs,
        out_shape=out_shape,
        input_output_aliases=aliases,
        scratch_shapes=[pltpu.VMEM((tm, D_MODEL), F32)],
        compiler_params=_params(1, vmem=VMEM_LIMIT if roll is None else VMEM_LIMIT_FULL),
        name="ffn",
    )(*args)
    return outs[0] if roll is None else (outs[0], outs[1:1 + n_roll], outs[1 + n_roll:])


def _head_rmsnorm(x, g):
    low = lax.broadcasted_iota(jnp.int32, (1, LANES), 1) < HEAD_DIM
    outs = []
    for p in range(GROUP_WIDTH // LANES):
        xp = x[:, p * LANES:(p + 1) * LANES]
        sq = xp * xp
        s_all = jnp.sum(sq, axis=-1, keepdims=True)
        s_low = jnp.sum(jnp.where(low, sq, 0.0), axis=-1, keepdims=True)
        ms = jnp.where(low, s_low, s_all - s_low) * (1.0 / HEAD_DIM)
        outs.append(xp * lax.rsqrt(ms + EPS))
    return jnp.concatenate(outs, axis=-1) * g


def _in_kernel(x_ref, g_ref, w_ref, qn_ref, kn_ref, *refs, tm, dils, kv_from, kv_rows, row_kv):
    u_ref, gate_ref = refs[0:2]
    qkv_refs = refs[2:5]
    kvt_refs = refs[5:8]
    kvrow_refs = refs[8:11] if row_kv else None
    stage_ref, keep_ref = refs[-2:]
    j = pl.program_id(1)
    h = _rmsnorm(x_ref[...], g_ref[...]).astype(BF16)
    gate_ref[...] = jax.nn.sigmoid(_mm(h, w_ref[:, GATE_OFF:IN_WIDTH])).astype(gate_ref.dtype)
    for g in range(N_GROUPS):
        c0, c1 = g * GROUP_WIDTH, (g + 1) * GROUP_WIDTH
        q = _head_rmsnorm(_mm(h, w_ref[:, Q_OFF + c0:Q_OFF + c1]), qn_ref[:, c0:c1])
        k = _head_rmsnorm(_mm(h, w_ref[:, K_OFF + c0:K_OFF + c1]), kn_ref[:, c0:c1])
        v = _mm(h, w_ref[:, V_OFF + c0:V_OFF + c1])
        d = dils[g]
        if d == 1:
            qkv_refs[g][:, 0:GROUP_WIDTH] = q.astype(BF16)
            qkv_refs[g][:, GROUP_WIDTH:2 * GROUP_WIDTH] = k.astype(BF16)
            qkv_refs[g][:, 2 * GROUP_WIDTH:3 * GROUP_WIDTH] = v.astype(BF16)
        else:
            for ci, a in enumerate((q, k, v)):
                for cl in range(GROUP_WIDTH // LANES):
                    stage_ref[ci * (GROUP_WIDTH // LANES) + cl] = a[:, cl * LANES:(cl + 1) * LANES]
            for r in range(d):
                for cl in range(3 * GROUP_WIDTH // LANES):
                    qkv_refs[g][0, r, :, cl * LANES:(cl + 1) * LANES] = (
                        stage_ref[cl, pl.ds(r, tm // d, stride=d), :].astype(BF16))
        if row_kv:
            kvrow_refs[g][:, 0:GROUP_WIDTH] = k
            kvrow_refs[g][:, GROUP_WIDTH:2 * GROUP_WIDTH] = v

        r0 = kv_rows[g]
        if kv_from[g] == 0:
            kvt_refs[g][0, 0:GROUP_WIDTH, :] = k[r0:, :].T
            kvt_refs[g][0, GROUP_WIDTH:2 * GROUP_WIDTH, :] = v[r0:, :].T
        else:
            keep_ref[g, 0, 0:tm - r0, :] = k[r0:, :]
            keep_ref[g, 1, 0:tm - r0, :] = v[r0:, :]
    u_ref[...] = _mm(h, w_ref[:, 0:POOL_WIDTH])
    for g in range(N_GROUPS):
        if kv_from[g] > 0:
            @pl.when(j >= kv_from[g])
            def _(g=g):
                rows = tm - kv_rows[g]
                kvt_refs[g][0, 0:GROUP_WIDTH, :] = keep_ref[g, 0, 0:rows, :].T
                kvt_refs[g][0, GROUP_WIDTH:2 * GROUP_WIDTH, :] = keep_ref[g, 1, 0:rows, :].T


def _inproj(x1, g, w_in, qn, kn, n_seq, seq, tm, keeps, dils, row_kv):
    t = n_seq * seq
    nj = seq // tm
    row = lambda width: pl.BlockSpec((tm, width), lambda b, j: (b * nj + j, 0))
    qkv_specs, qkv_shapes = [], []
    for d in dils:
        if d == 1:
            qkv_specs.append(row(3 * GROUP_WIDTH))
            qkv_shapes.append(jax.ShapeDtypeStruct((t, 3 * GROUP_WIDTH), BF16))
        else:
            qkv_specs.append(pl.BlockSpec((1, d, tm // d, 3 * GROUP_WIDTH), lambda b, j: (b, 0, j, 0)))
            qkv_shapes.append(jax.ShapeDtypeStruct((n_seq, d, seq // d, 3 * GROUP_WIDTH), BF16))
    kv_from, kv_rows, kvt_specs, kvt_shapes = [], [], [], []
    for keep in keeps:
        tw = min(tm, keep)
        nb = keep // tw
        kv_from.append(nj - nb)
        kv_rows.append(tm - tw)
        kvt_specs.append(pl.BlockSpec(
            (1, 2 * GROUP_WIDTH, tw), lambda b, j, nb=nb: (b, 0, jnp.maximum(j - (nj - nb), 0))))
        kvt_shapes.append(jax.ShapeDtypeStruct((n_seq, 2 * GROUP_WIDTH, keep), F32))
    keep_rows = max([tm - r for r, f in zip(kv_rows, kv_from) if f > 0], default=8)
    out_specs = [row(POOL_WIDTH), row(2 * D_MODEL)] + qkv_specs + kvt_specs
    out_shape = [jax.ShapeDtypeStruct((t, POOL_WIDTH), F32),
                 jax.ShapeDtypeStruct((t, 2 * D_MODEL), BF16)] + qkv_shapes + kvt_shapes
    if row_kv:
        out_specs += [row(2 * GROUP_WIDTH)] * N_GROUPS
        out_shape += [jax.ShapeDtypeStruct((t, 2 * GROUP_WIDTH), F32)] * N_GROUPS
    return pl.pallas_call(
        functools.partial(_in_kernel, tm=tm, dils=tuple(dils), kv_from=tuple(kv_from),
                          kv_rows=tuple(kv_rows), row_kv=row_kv),
        grid=(n_seq, nj),
        in_specs=[
            row(D_MODEL),
            _resident((1, D_MODEL)),
            _resident((D_MODEL, IN_WIDTH)),
            _resident((1, ATT_WIDTH)),
            _resident((1, ATT_WIDTH)),
        ],
        out_specs=out_specs,
        out_shape=out_shape,
        scratch_shapes=[pltpu.VMEM((3 * GROUP_WIDTH // LANES, tm, LANES), F32),
                        pltpu.VMEM((N_GROUPS, 2, keep_rows, GROUP_WIDTH), F32)],
        compiler_params=_params(2, vmem=VMEM_LIMIT_FULL),
        name="inproj",
    )(x1, g, w_in, qn, kn)


def _attn_kernel(qkv_ref, o_ref, l_ref, *, n_blocks):
    n_streams = qkv_ref.shape[1]
    i = lax.broadcasted_iota(jnp.int32, (ATT_BLK, 2 * ATT_BLK), 0)
    j = lax.broadcasted_iota(jnp.int32, (ATT_BLK, 2 * ATT_BLK), 1)
    band = (j >= i) & (j <= i + ATT_BLK)
    first = band[:, ATT_BLK:]
    low = lax.broadcasted_iota(jnp.int32, (ATT_BLK, LANES), 1) < HEAD_DIM
    scale = jnp.asarray(HEAD_DIM ** -0.5, BF16)

    def run(blocks):
        n_pairs = GROUP_WIDTH // LANES
        scores = []
        for r, q0, k0, n_keys, mask in blocks:
            for p in range(n_pairs):
                c = p * LANES
                qp = qkv_ref[0, r, pl.ds(q0, ATT_BLK), c:c + LANES] * scale
                kp = qkv_ref[0, r, pl.ds(k0, n_keys), GROUP_WIDTH + c:GROUP_WIDTH + c + LANES]
                for own in (low, ~low):
                    s = _mm_nt(jnp.where(own, qp, jnp.zeros_like(qp)), kp)
                    scores.append(jnp.where(mask, s, -jnp.inf))
        probs, inv_den, lse = [], [], []
        for s in scores:
            mx = jnp.max(s, axis=-1, keepdims=True)
            e = jnp.exp(s - mx)
            den = jnp.sum(e, axis=-1, keepdims=True)
            probs.append(e.astype(BF16))
            inv_den.append(1.0 / den)
            lse.append(mx + jnp.log(den))
        for bi, (r, q0, k0, n_keys, _) in enumerate(blocks):
            for p in range(n_pairs):
                c = p * LANES
                h0 = 2 * (bi * n_pairs + p)
                vp = qkv_ref[0, r, pl.ds(k0, n_keys), 2 * GROUP_WIDTH + c:2 * GROUP_WIDTH + c + LANES]
                o_lo = _mm(probs[h0], vp) * inv_den[h0]
                o_hi = _mm(probs[h0 + 1], vp) * inv_den[h0 + 1]
                o_ref[0, r, pl.ds(q0, ATT_BLK), c:c + LANES] = jnp.where(low, o_lo, o_hi).astype(o_ref.dtype)
                l_ref[0, r, pl.ds(q0, ATT_BLK), c:c + LANES] = jnp.where(low, lse[h0], lse[h0 + 1])

    def first_block(r):
        return (r, 0, 0, ATT_BLK, first)

    def band_block(r, n):
        q0, k0 = n * ATT_BLK, (n - 1) * ATT_BLK
        if not isinstance(n, int):
            q0, k0 = pl.multiple_of(q0, ATT_BLK), pl.multiple_of(k0, ATT_BLK)
        return (r, q0, k0, 2 * ATT_BLK, band)

    if n_blocks == 1:
        def body(it, carry):
            run([first_block(2 * it), first_block(2 * it + 1)])
            return carry
        lax.fori_loop(0, n_streams // 2, body, 0)
    else:
        half = n_blocks // 2

        def body(it, carry):
            r = it // half
            m = it % half

            @pl.when(m == 0)
            def _():
                run([first_block(r), band_block(r, 1)])

            @pl.when(m > 0)
            def _():
                run([band_block(r, 2 * m), band_block(r, 2 * m + 1)])
            return carry
        lax.fori_loop(0, n_streams * half, body, 0)


def _attn_prompt(qkv, n_seq, seq, dil):
    ln = seq // dil
    out = jax.ShapeDtypeStruct((n_seq, dil, ln, GROUP_WIDTH), F32)
    ospec = pl.BlockSpec((1, dil, ln, GROUP_WIDTH), lambda b: (b, 0, 0, 0))
    return pl.pallas_call(
        functools.partial(_attn_kernel, n_blocks=ln // ATT_BLK),
        grid=(n_seq,),
        in_specs=[pl.BlockSpec((1, dil, ln, 3 * GROUP_WIDTH), lambda b: (b, 0, 0, 0))],
        out_specs=[ospec, ospec],
        out_shape=[jax.ShapeDtypeStruct(out.shape, BF16), out],
        compiler_params=_params(1),
        name=f"attn_d{dil}",
    )(qkv)


def _combine_groups(os_, ls):
    mx = jnp.maximum(jnp.maximum(ls[0], ls[1]), ls[2])
    es = [jnp.exp(l - mx) for l in ls]
    inv = 1.0 / (es[0] + es[1] + es[2])
    return (es[0] * inv) * os_[0] + (es[1] * inv) * os_[1] + (es[2] * inv) * os_[2]


def _merge_tail(x1, pooled, att, gate, pw_ref, ps_ref, wbp_ref, wba_ref, wo_ref):
    mixed = jnp.concatenate(
        [_mm(pooled[gi].astype(BF16), pw_ref[gi]) for gi in range(len(POOL_WINDOWS))], axis=-1)
    pool_y = (mixed * ps_ref[...]).astype(BF16)
    merged = (gate[:, 0:D_MODEL] * _mm(pool_y, wbp_ref[...])
              + gate[:, D_MODEL:2 * D_MODEL] * _mm(att.astype(BF16), wba_ref[...]))
    return x1 + _mm(merged.astype(BF16), wo_ref[...])


def _merge_prompt_kernel(x1_ref, u_ref, up_ref, o0_ref, o1_ref, o2_ref, l0_ref, l1_ref, l2_ref,
                         gate_ref, pw_ref, ps_ref, wbp_ref, wba_ref, wo_ref, y_ref,
                         ext_ref, nat_ref, *, tm):
    j = pl.program_id(1)
    u = u_ref[...]
    ext_ref[0:16, :] = jnp.where(j > 0, up_ref[...], 0.0)
    ext_ref[16:16 + tm, :] = u
    pos = j * tm + lax.broadcasted_iota(jnp.int32, (tm, 1), 0)
    pooled = []
    for gi, w in enumerate(POOL_WINDOWS):
        c0 = gi * POOL_GROUP_WIDTH
        acc = u[:, c0:c0 + POOL_GROUP_WIDTH]
        for k in range(1, w):
            acc = acc + ext_ref[16 - k:16 - k + tm, c0:c0 + POOL_GROUP_WIDTH]
        cnt = jnp.minimum(pos + 1, w).astype(F32)
        pooled.append(acc / cnt - u[:, c0:c0 + POOL_GROUP_WIDTH])

    def natural(ref, slot):
        if len(ref.shape) == 2:
            return ref[...].astype(F32)
        d = ref.shape[1]
        n_tiles = GROUP_WIDTH // LANES
        for r in range(d):
            for cl in range(n_tiles):
                nat_ref[slot * n_tiles + cl, pl.ds(r, tm // d, stride=d), :] = (
                    ref[0, r, :, cl * LANES:(cl + 1) * LANES].astype(F32))
        return jnp.concatenate([nat_ref[slot * n_tiles + cl] for cl in range(n_tiles)], axis=-1)

    att = _combine_groups((natural(o0_ref, 0), natural(o1_ref, 1), natural(o2_ref, 2)),
                          (natural(l0_ref, 3), natural(l1_ref, 4), natural(l2_ref, 5)))
    y_ref[...] = _merge_tail(x1_ref[...], pooled, att, gate_ref[...],
                             pw_ref, ps_ref, wbp_ref, wba_ref, wo_ref)


def _merge_prompt(x1, u, os_, ls, gate, pw, ps, wbp, wba, wo, n_seq, seq, tm):
    t = n_seq * seq
    nj = seq // tm
    row = lambda width: pl.BlockSpec((tm, width), lambda b, j: (b * nj + j, 0))
    prev = pl.BlockSpec((16, POOL_WIDTH),
                        lambda b, j: (jnp.maximum((b * nj + j) * (tm // 16) - 1, 0), 0))
    att_args, att_specs = [], []
    for a in list(os_) + list(ls):
        d = a.shape[1]
        if d == 1:
            att_args.append(a.reshape(t, GROUP_WIDTH))
            att_specs.append(row(GROUP_WIDTH))
        else:
            att_args.append(a)
            att_specs.append(pl.BlockSpec((1, d, tm // d, GROUP_WIDTH), lambda b, j: (b, 0, j, 0)))
    return pl.pallas_call(
        functools.partial(_merge_prompt_kernel, tm=tm),
        grid=(n_seq, nj),
        in_specs=[row(D_MODEL), row(POOL_WIDTH), prev] + att_specs + [row(2 * D_MODEL)]
        + [_resident(a.shape) for a in (pw, ps, wbp, wba, wo)],
        out_specs=row(D_MODEL),
        out_shape=jax.ShapeDtypeStruct((t, D_MODEL), F32),
        scratch_shapes=[pltpu.VMEM((tm + 16, POOL_WIDTH), F32),
                        pltpu.VMEM((2 * N_GROUPS * GROUP_WIDTH // LANES, tm, LANES), F32)],
        compiler_params=_params(2),
        name="merge_prompt",
    )(x1, u, u, *att_args, gate, pw, ps, wbp, wba, wo)


def _merge_sample_kernel(x1_ref, ctx_ref, att_ref, gate_ref, pw_ref, ps_ref, wbp_ref, wba_ref,
                         wo_ref, y_ref, *, t_new):
    n_seq = ctx_ref.shape[0]
    for t in range(t_new):
        def ctx_row(r, c0):
            return ctx_ref[:, r * POOL_WIDTH + c0:r * POOL_WIDTH + c0 + POOL_GROUP_WIDTH]
        pooled = []
        for gi, w in enumerate(POOL_WINDOWS):
            c0 = gi * POOL_GROUP_WIDTH
            last = POOL_STATE + t
            acc = ctx_row(last, c0)
            for k in range(1, w):
                acc = acc + ctx_row(last - k, c0)
            pooled.append(acc / float(min(w, last + 1)) - ctx_row(last, c0))
        rows = pl.ds(t * n_seq, n_seq)
        y_ref[rows, :] = _merge_tail(x1_ref[rows, :], pooled, att_ref[rows, :], gate_ref[rows, :],
                                     pw_ref, ps_ref, wbp_ref, wba_ref, wo_ref)


def _merge_sample(x1, ctx, att, gate, pw, ps, wbp, wba, wo, t_new):
    args = (x1, ctx, att, gate, pw, ps, wbp, wba, wo)
    return pl.pallas_call(
        functools.partial(_merge_sample_kernel, t_new=t_new),
        grid=(1,),
        in_specs=[pl.BlockSpec(a.shape, lambda i, nd=a.ndim: (0,) * nd) for a in args],
        out_specs=pl.BlockSpec(x1.shape, lambda i: (0, 0)),
        out_shape=jax.ShapeDtypeStruct(x1.shape, F32),
        compiler_params=_params(1),
        name="merge_sample",
    )(*args)


def _cache_to_cols(c):
    n_seq, w = c.shape[0], c.shape[1]
    return c.transpose(0, 2, 3, 4, 1).reshape(n_seq, 2 * GROUP_WIDTH, w)


def _cols_to_cache(ct):
    n_seq, _, w = ct.shape
    return ct.reshape(n_seq, 2, HEADS, HEAD_DIM, w).transpose(0, 4, 1, 2, 3)[None]


def kernel(x_prompt, x_sample, cache_kv_w128, cache_kv_w512, cache_kv_w2048, state_pool,
           ffn1_norm, ffn1_w_gu, ffn1_w_down, mix_norm, w_in, q_norm, k_norm, pool_w,
           pool_scale, w_branch_pool, w_branch_att, w_out, ffn2_norm, ffn2_w_gu, ffn2_w_down):
    depth = ffn1_norm.shape[0]
    assert depth == 1
    n_p, seq, _ = x_prompt.shape
    n_s, t_new, _ = x_sample.shape
    caches_t = [_cache_to_cols(c[0]) for c in (cache_kv_w128, cache_kv_w512, cache_kv_w2048)]
    dils = tuple(d for _, d in ATT_CONFIGS)

    g1 = ffn1_norm[0][None, :]
    g2 = ffn2_norm[0][None, :]
    gm = mix_norm[0][None, :]
    wgu1, wd1 = ffn1_w_gu[0].astype(BF16), ffn1_w_down[0].astype(BF16)
    wgu2, wd2 = ffn2_w_gu[0].astype(BF16), ffn2_w_down[0].astype(BF16)
    win = w_in[0].astype(BF16)
    qn = q_norm[0].reshape(1, ATT_WIDTH)
    kn = k_norm[0].reshape(1, ATT_WIDTH)
    pw = pool_w[0].astype(BF16)
    ps = pool_scale[0][None, :]
    wbp, wba, wo = (w_branch_pool[0].astype(BF16), w_branch_att[0].astype(BF16), w_out[0].astype(BF16))

    n_tok = n_s * t_new
    xs = x_sample.reshape(n_tok, D_MODEL)
    x1s = _ffn(xs, g1, wgu1, wd1, tm=n_tok)
    us, gates, sq0, sq1, sq2, st0, st1, st2, sk0, sk1, sk2 = _inproj(
        x1s, gm, win, qn, kn, 1, n_tok, 256, (n_tok,) * N_GROUPS, (1,) * N_GROUPS, row_kv=True)
    new_cols = [a[0] for a in (st0, st1, st2)]
    q_s = jnp.concatenate([a[:, 0:GROUP_WIDTH] for a in (sq0, sq1, sq2)], axis=-1).astype(F32)
    q_s = q_s.reshape(n_s, t_new, ATT_WIDTH)
    new_rows = [a.reshape(n_s, t_new, 2 * GROUP_WIDTH) for a in (sk0, sk1, sk2)]

    xp = x_prompt.reshape(n_p * seq, D_MODEL)
    tm_ffn = n_p * seq // n_s
    x1, half_rolled, probs_stats = _ffn(
        xp, g1, wgu1, wd1, tm_ffn, roll=(caches_t, new_cols, 0, t_new, None),
        stage_args=[q_s] + new_rows)
    keeps = tuple(min(w, seq) for w, _ in ATT_CONFIGS)
    u, gate, qkv0, qkv1, qkv2, kvt0, kvt1, kvt2 = _inproj(
        x1, gm, win, qn, kn, n_p, seq, 512, keeps, dils, row_kv=False)
    os_, ls = [], []
    for qkv, dil in zip((qkv0, qkv1, qkv2), dils):
        o, l = _attn_prompt(qkv.reshape(n_p, dil, seq // dil, 3 * GROUP_WIDTH), n_p, seq, dil)
        os_.append(o)
        ls.append(l)
    x2 = _merge_prompt(x1, u, os_, ls, gate, pw, ps, wbp, wba, wo, n_p, seq, tm=512)
    y_p, new_caches_t, (att_s,) = _ffn(
        x2, g2, wgu2, wd2, tm_ffn, roll=(caches_t, new_cols, 1, t_new, half_rolled),
        stage_args=list(probs_stats) + new_rows)
    y_prompt = y_p.reshape(n_p, seq, D_MODEL)
    new_kv_p = [_cols_to_cache(kvt) for kvt in (kvt0, kvt1, kvt2)]
    new_pool_p = u.reshape(n_p, seq, POOL_WIDTH)[None, :, seq - POOL_STATE:, :]
    new_kv_s = [_cols_to_cache(c) for c in new_caches_t]

    u_ctx = jnp.concatenate([state_pool[0], us.reshape(n_s, t_new, POOL_WIDTH)], axis=1)
    new_pool_s = u_ctx[None, :, t_new:, :]
    to_ts = lambda a: a.reshape(n_s, t_new, -1).transpose(1, 0, 2).reshape(n_tok, -1)
    x2s = _merge_sample(to_ts(x1s), u_ctx.reshape(n_s, (POOL_STATE + t_new) * POOL_WIDTH),
                        to_ts(att_s), to_ts(gates), pw, ps, wbp, wba, wo, t_new)
    y_s = _ffn(x2s, g2, wgu2, wd2, tm=n_tok)
    y_sample = y_s.reshape(t_new, n_s, D_MODEL).transpose(1, 0, 2)

    return (y_prompt, y_sample, new_kv_p[0], new_kv_p[1], new_kv_p[2], new_pool_p,
            new_kv_s[0], new_kv_s[1], new_kv_s[2], new_pool_s)
```

```python
import functools

import jax
import jax.numpy as jnp
from jax import lax
from jax.experimental import pallas as pl
from jax.experimental.pallas import tpu as pltpu

F32 = jnp.float32
BF16 = jnp.bfloat16

D_MODEL = 1024
D_FF = 2816
POOL_WINDOWS = (2, 4, 8, 16)
POOL_GROUP_WIDTH = 128
POOL_WIDTH = 512
POOL_STATE = 15
POOL_PAD = 16
POOL_LEVELS = 3
ATT_CONFIGS = ((128, 1), (512, 4), (2048, 16))
N_GROUPS = 3
HEADS = 8
HEAD_DIM = 64
GROUP_WIDTH = HEADS * HEAD_DIM
ATT_WIDTH = N_GROUPS * GROUP_WIDTH
ATT_BLK = 128
EPS = 1e-6
Q_OFF = POOL_WIDTH
K_OFF = Q_OFF + ATT_WIDTH
V_OFF = K_OFF + ATT_WIDTH
GATE_OFF = V_OFF + ATT_WIDTH
IN_WIDTH = GATE_OFF + 2 * D_MODEL

BLOCKS_PER_TRIP = 2
LANES = 128
BF16_ROWS = 16
FF_CHUNK = 256
VMEM_LIMIT = 56 * 1024 * 1024
VMEM_LIMIT_FULL = 62 * 1024 * 1024


def _params(n_axes, vmem=VMEM_LIMIT):
    return pltpu.CompilerParams(
        dimension_semantics=("arbitrary",) * n_axes, vmem_limit_bytes=vmem)


def _resident(shape):
    nd = len(shape)
    return pl.BlockSpec(shape, lambda *_: (0,) * nd, pipeline_mode=pl.Buffered(1))


def _rmsnorm(x, g):
    ms = jnp.mean(x * x, axis=-1, keepdims=True)
    return x * lax.rsqrt(ms + EPS) * g


def _mm(a, b):
    return jnp.dot(a, b, preferred_element_type=F32)


def _mm_nt(a, b):
    return lax.dot_general(a, b, (((1,), (1,)), ((), ())), preferred_element_type=F32)


def _roll_cache_half(c_ref, kvt_ref, n_ref, seq, t_new):
    w = c_ref.shape[2]
    tail_lane = lax.broadcasted_iota(jnp.int32, (GROUP_WIDTH, LANES), 1) >= LANES - t_new
    shift = (LANES - t_new) - t_new * (seq % (LANES // t_new))
    new_cols = pltpu.roll(kvt_ref[...], shift, axis=1)
    rolled = pltpu.roll(c_ref[0], w - t_new, axis=1)
    if w > LANES:
        n_ref[0, :, 0:w - LANES] = rolled[:, 0:w - LANES]
    n_ref[0, :, w - LANES:w] = jnp.where(tail_lane, new_cols, rolled[:, w - LANES:w])


def _sample_rows(t_new):
    rows = t_new * HEADS
    r_t = lax.broadcasted_iota(jnp.int32, (rows, 1), 0) // HEADS
    r_h = lax.broadcasted_iota(jnp.int32, (rows, GROUP_WIDTH), 0) % HEADS
    diag = (lax.broadcasted_iota(jnp.int32, (rows, GROUP_WIDTH), 1) // HEAD_DIM) == r_h
    return rows, r_t, diag


def _sample_scores(q_ref, kvn_refs, k_refs, p_refs, st_refs, t_new):
    rows, r_t, diag = _sample_rows(t_new)
    lane = lax.broadcasted_iota(jnp.int32, (rows, LANES), 1)
    for g in range(N_GROUPS):
        dil = ATT_CONFIGS[g][1]
        w = k_refs[g].shape[2]
        kn = kvn_refs[g][0][:, 0:GROUP_WIDTH].astype(BF16).astype(F32)
        qg = q_ref[0, :, g * GROUP_WIDTH:(g + 1) * GROUP_WIDTH].astype(BF16).astype(F32)
        qf = jnp.concatenate(
            [jnp.broadcast_to(qg[t:t + 1, :], (HEADS, GROUP_WIDTH)) for t in range(t_new)], axis=0)
        qf = jnp.where(diag, qf, 0.0)
        wi = lax.broadcasted_iota(jnp.int32, (rows, w), 1)
        ok_c = (wi >= r_t) & (((wi - r_t) & (dil - 1)) == 0)
        s_c = _mm(qf.astype(BF16), k_refs[g][0].astype(BF16)) * (HEAD_DIM ** -0.5)
        s_c = jnp.where(ok_c, s_c, -jnp.inf)
        s_n = []
        for t2 in range(t_new):
            ok = (r_t >= t2) & (((r_t - t2) & (dil - 1)) == 0)
            s = jnp.sum(qf * kn[t2:t2 + 1, :], axis=-1, keepdims=True) * (HEAD_DIM ** -0.5)
            s_n.append(jnp.where(ok, s, -jnp.inf))
        mx = jnp.max(s_c, axis=-1, keepdims=True)
        for s in s_n:
            mx = jnp.maximum(mx, s)
        e_c = jnp.exp(s_c - mx)
        e_n = [jnp.exp(s - mx) for s in s_n]
        den = jnp.sum(e_c, axis=-1, keepdims=True)
        for e in e_n:
            den = den + e
        inv = 1.0 / den
        p_refs[g][0] = (e_c * inv).astype(BF16)
        st = jnp.where(lane == t_new, mx + jnp.log(den), 0.0)
        for t2 in range(t_new):
            st = jnp.where(lane == t2, e_n[t2] * inv, st)
        st_refs[g][0] = st


def _sample_values(p_refs, st_refs, kvn_refs, v_refs, att_ref, t_new):
    rows, _, diag = _sample_rows(t_new)
    os_, ls = [], []
    for g in range(N_GROUPS):
        st = st_refs[g][0]
        vn = kvn_refs[g][0][:, GROUP_WIDTH:2 * GROUP_WIDTH].astype(BF16).astype(F32)
        o = _mm_nt(p_refs[g][0], v_refs[g][0].astype(BF16))
        for t2 in range(t_new):
            o = o + st[:, t2:t2 + 1].astype(BF16).astype(F32) * vn[t2:t2 + 1, :]
        os_.append(o)
        ls.append(st[:, t_new:t_new + 1])
    full = jnp.where(diag, _combine_groups(os_, ls), 0.0)
    att_ref[0] = jnp.concatenate(
        [jnp.sum(full[t * HEADS:(t + 1) * HEADS, :], axis=0, keepdims=True) for t in range(t_new)],
        axis=0)


def _ffn_kernel(x_ref, g_ref, wgu_ref, wd_ref, *refs, n_roll, aliased, t_new, stage):
    it = iter(refs)
    take = lambda n: [next(it) for _ in range(n)]
    c_refs, kvt_refs = take(n_roll), take(n_roll)
    take(n_roll if aliased else 0)
    if stage == "scores":
        q_ref, kvn_refs = next(it), take(n_roll)
    elif stage == "values":
        p_refs, st_refs, kvn_refs = take(n_roll), take(n_roll), take(n_roll)
    o_ref = next(it)
    n_refs = take(n_roll)
    if stage == "scores":
        _sample_scores(q_ref, kvn_refs, c_refs, take(n_roll), take(n_roll), t_new)
    elif stage == "values":
        _sample_values(p_refs, st_refs, kvn_refs, c_refs, next(it), t_new)
    acc_ref = next(it)
    for g in range(n_roll):
        _roll_cache_half(c_refs[g], kvt_refs[g], n_refs[g], pl.program_id(0), t_new)
    x = x_ref[...]
    xn = _rmsnorm(x, g_ref[...]).astype(BF16)
    for c, lo in enumerate(range(0, D_FF, FF_CHUNK)):
        hi = min(lo + FF_CHUNK, D_FF)
        a = _mm(xn, wgu_ref[:, lo:hi])
        b = _mm(xn, wgu_ref[:, D_FF + lo:D_FF + hi])
        h = (a * jax.nn.sigmoid(a) * b).astype(BF16)
        d = _mm(h, wd_ref[lo:hi, :])
        if c == 0:
            acc_ref[...] = d
        else:
            acc_ref[...] += d
    o_ref[...] = x + 0.5 * acc_ref[...]


def _ffn(x, g, wgu, wd, tm, roll=None, stage_args=()):
    t = x.shape[0]
    steps = t // tm
    in_specs = [
        pl.BlockSpec((tm, D_MODEL), lambda i: (i, 0)),
        _resident((1, D_MODEL)),
        _resident((D_MODEL, 2 * D_FF)),
        _resident((D_FF, D_MODEL)),
    ]
    args = [x, g, wgu, wd]
    out_specs = [pl.BlockSpec((tm, D_MODEL), lambda i: (i, 0))]
    out_shape = [jax.ShapeDtypeStruct((t, D_MODEL), F32)]
    aliases = {}
    n_roll, t_new, stage = 0, 0, None
    if roll is not None:
        caches_t, kvts, half, t_new, partial = roll
        n_roll = len(caches_t)
        assert all(c.shape[0] == steps for c in caches_t)
        per_tile = LANES // t_new
        half_specs = [pl.BlockSpec((1, GROUP_WIDTH, c.shape[2]), lambda i: (i, half, 0))
                      for c in caches_t]
        in_specs += half_specs
        in_specs += [pl.BlockSpec((GROUP_WIDTH, LANES), lambda i: (half, i // per_tile))] * n_roll
        args += list(caches_t) + list(kvts)
        if partial is not None:
            aliases = {len(args) + k: 1 + k for k in range(n_roll)}
            in_specs += [pl.BlockSpec(memory_space=pl.ANY)] * n_roll
            args += list(partial)
        out_specs += half_specs
        out_shape += [jax.ShapeDtypeStruct(c.shape, F32) for c in caches_t]
        stage = ("scores", "values")[half]
        per_seq = lambda a: pl.BlockSpec((1,) + a.shape[1:], lambda i: (i, 0, 0))
        in_specs += [per_seq(a) for a in stage_args]
        args += list(stage_args)
        n_seq, rows = steps, t_new * HEADS
        if stage == "scores":
            stage_out = ([jax.ShapeDtypeStruct((n_seq, rows, c.shape[2]), BF16) for c in caches_t]
                         + [jax.ShapeDtypeStruct((n_seq, rows, LANES), F32)] * n_roll)
        else:
            stage_out = [jax.ShapeDtypeStruct((n_seq, t_new, GROUP_WIDTH), F32)]
        out_specs += [per_seq(a) for a in stage_out]
        out_shape += stage_out
    outs = pl.pallas_call(
        functools.partial(_ffn_kernel, n_roll=n_roll, aliased=bool(aliases), t_new=t_new,
                          stage=stage),
        grid=(steps,),
        in_specs=in_specs,
        out_specs=out_specs,
        out_shape=out_shape,
        input_output_aliases=aliases,
        scratch_shapes=[pltpu.VMEM((tm, D_MODEL), F32)],
        compiler_params=_params(1, vmem=VMEM_LIMIT if roll is None else VMEM_LIMIT_FULL),
        name="ffn",
    )(*args)
    return outs[0] if roll is None else (outs[0], outs[1:1 + n_roll], outs[1 + n_roll:])


def _head_rmsnorm(x, g):
    low = lax.broadcasted_iota(jnp.int32, (1, LANES), 1) < HEAD_DIM
    outs = []
    for p in range(GROUP_WIDTH // LANES):
        xp = x[:, p * LANES:(p + 1) * LANES]
        sq = xp * xp
        s_all = jnp.sum(sq, axis=-1, keepdims=True)
        s_low = jnp.sum(jnp.where(low, sq, 0.0), axis=-1, keepdims=True)
        ms = jnp.where(low, s_low, s_all - s_low) * (1.0 / HEAD_DIM)
        outs.append(xp * lax.rsqrt(ms + EPS))
    return jnp.concatenate(outs, axis=-1) * g


def _in_kernel(x_ref, g_ref, w_ref, qn_ref, kn_ref, *refs, tm, dils, kv_from, kv_rows, row_kv):
    u_ref, gate_ref = refs[0:2]
    qkv_refs = refs[2:5]
    kvt_refs = refs[5:8]
    kvrow_refs = refs[8:11] if row_kv else None
    stage_ref, keep_ref = refs[-2:]
    j = pl.program_id(1)
    h = _rmsnorm(x_ref[...], g_ref[...]).astype(BF16)
    gate_ref[...] = jax.nn.sigmoid(_mm(h, w_ref[:, GATE_OFF:IN_WIDTH])).astype(gate_ref.dtype)
    for g in range(N_GROUPS):
        c0, c1 = g * GROUP_WIDTH, (g + 1) * GROUP_WIDTH
        q = _head_rmsnorm(_mm(h, w_ref[:, Q_OFF + c0:Q_OFF + c1]), qn_ref[:, c0:c1])
        k = _head_rmsnorm(_mm(h, w_ref[:, K_OFF + c0:K_OFF + c1]), kn_ref[:, c0:c1])
        v = _mm(h, w_ref[:, V_OFF + c0:V_OFF + c1])
        d = dils[g]
        if d == 1:
            qkv_refs[g][:, 0:GROUP_WIDTH] = q.astype(BF16)
            qkv_refs[g][:, GROUP_WIDTH:2 * GROUP_WIDTH] = k.astype(BF16)
            qkv_refs[g][:, 2 * GROUP_WIDTH:3 * GROUP_WIDTH] = v.astype(BF16)
        else:
            for ci, a in enumerate((q, k, v)):
                for cl in range(GROUP_WIDTH // LANES):
                    stage_ref[ci * (GROUP_WIDTH // LANES) + cl] = a[:, cl * LANES:(cl + 1) * LANES]
            for r in range(d):
                for cl in range(3 * GROUP_WIDTH // LANES):
                    qkv_refs[g][0, r, :, cl * LANES:(cl + 1) * LANES] = (
                        stage_ref[cl, pl.ds(r, tm // d, stride=d), :].astype(BF16))
        if row_kv:
            kvrow_refs[g][:, 0:GROUP_WIDTH] = k
            kvrow_refs[g][:, GROUP_WIDTH:2 * GROUP_WIDTH] = v

        r0 = kv_rows[g]
        if kv_from[g] == 0:
            kvt_refs[g][0, 0:GROUP_WIDTH, :] = k[r0:, :].T
            kvt_refs[g][0, GROUP_WIDTH:2 * GROUP_WIDTH, :] = v[r0:, :].T
        else:
            keep_ref[g, 0, 0:tm - r0, :] = k[r0:, :]
            keep_ref[g, 1, 0:tm - r0, :] = v[r0:, :]
    u_ref[...] = _mm(h, w_ref[:, 0:POOL_WIDTH])
    for g in range(N_GROUPS):
        if kv_from[g] > 0:
            @pl.when(j >= kv_from[g])
            def _(g=g):
                rows = tm - kv_rows[g]
                kvt_refs[g][0, 0:GROUP_WIDTH, :] = keep_ref[g, 0, 0:rows, :].T
                kvt_refs[g][0, GROUP_WIDTH:2 * GROUP_WIDTH, :] = keep_ref[g, 1, 0:rows, :].T


def _inproj(x1, g, w_in, qn, kn, n_seq, seq, tm, keeps, dils, row_kv):
    t = n_seq * seq
    nj = seq // tm
    row = lambda width: pl.BlockSpec((tm, width), lambda b, j: (b * nj + j, 0))
    qkv_specs, qkv_shapes = [], []
    for d in dils:
        if d == 1:
            qkv_specs.append(row(3 * GROUP_WIDTH))
            qkv_shapes.append(jax.ShapeDtypeStruct((t, 3 * GROUP_WIDTH), BF16))
        else:
            qkv_specs.append(pl.BlockSpec((1, d, tm // d, 3 * GROUP_WIDTH), lambda b, j: (b, 0, j, 0)))
            qkv_shapes.append(jax.ShapeDtypeStruct((n_seq, d, seq // d, 3 * GROUP_WIDTH), BF16))
    kv_from, kv_rows, kvt_specs, kvt_shapes = [], [], [], []
    for keep in keeps:
        tw = min(tm, keep)
        nb = keep // tw
        kv_from.append(nj - nb)
        kv_rows.append(tm - tw)
        kvt_specs.append(pl.BlockSpec(
            (1, 2 * GROUP_WIDTH, tw), lambda b, j, nb=nb: (b, 0, jnp.maximum(j - (nj - nb), 0))))
        kvt_shapes.append(jax.ShapeDtypeStruct((n_seq, 2 * GROUP_WIDTH, keep), F32))
    keep_rows = max([tm - r for r, f in zip(kv_rows, kv_from) if f > 0], default=8)
    out_specs = [row(POOL_WIDTH), row(2 * D_MODEL)] + qkv_specs + kvt_specs
    out_shape = [jax.ShapeDtypeStruct((t, POOL_WIDTH), F32),
                 jax.ShapeDtypeStruct((t, 2 * D_MODEL), BF16)] + qkv_shapes + kvt_shapes
    if row_kv:
        out_specs += [row(2 * GROUP_WIDTH)] * N_GROUPS
        out_shape += [jax.ShapeDtypeStruct((t, 2 * GROUP_WIDTH), F32)] * N_GROUPS
    return pl.pallas_call(
        functools.partial(_in_kernel, tm=tm, dils=tuple(dils), kv_from=tuple(kv_from),
                          kv_rows=tuple(kv_rows), row_kv=row_kv),
        grid=(n_seq, nj),
        in_specs=[
            row(D_MODEL),
            _resident((1, D_MODEL)),
            _resident((D_MODEL, IN_WIDTH)),
            _resident((1, ATT_WIDTH)),
            _resident((1, ATT_WIDTH)),
        ],
        out_specs=out_specs,
        out_shape=out_shape,
        scratch_shapes=[pltpu.VMEM((3 * GROUP_WIDTH // LANES, tm, LANES), F32),
                        pltpu.VMEM((N_GROUPS, 2, keep_rows, GROUP_WIDTH), F32)],
        compiler_params=_params(2, vmem=VMEM_LIMIT_FULL),
        name="inproj",
    )(x1, g, w_in, qn, kn)


def _attn_kernel(qkv_ref, o_ref, l_ref, *, n_blocks):
    n_streams = qkv_ref.shape[1]
    i = lax.broadcasted_iota(jnp.int32, (ATT_BLK, 2 * ATT_BLK), 0)
    j = lax.broadcasted_iota(jnp.int32, (ATT_BLK, 2 * ATT_BLK), 1)
    band = (j >= i) & (j <= i + ATT_BLK)
    first = band[:, ATT_BLK:]
    low = lax.broadcasted_iota(jnp.int32, (ATT_BLK, LANES), 1) < HEAD_DIM
    scale = jnp.asarray(HEAD_DIM ** -0.5, BF16)

    def run(blocks):
        n_pairs = GROUP_WIDTH // LANES
        scores = []
        for r, q0, k0, n_keys, mask in blocks:
            for p in range(n_pairs):
                c = p * LANES
                qp = qkv_ref[0, r, pl.ds(q0, ATT_BLK), c:c + LANES] * scale
                kp = qkv_ref[0, r, pl.ds(k0, n_keys), GROUP_WIDTH + c:GROUP_WIDTH + c + LANES]
                for own in (low, ~low):
                    s = _mm_nt(jnp.where(own, qp, jnp.zeros_like(qp)), kp)
                    scores.append(jnp.where(mask, s, -jnp.inf))
        probs, inv_den, lse = [], [], []
        for s in scores:
            mx = jnp.max(s, axis=-1, keepdims=True)
            e = jnp.exp(s - mx)
            den = jnp.sum(e, axis=-1, keepdims=True)
            probs.append(e.astype(BF16))
            inv_den.append(1.0 / den)
            lse.append(mx + jnp.log(den))
        for bi, (r, q0, k0, n_keys, _) in enumerate(blocks):
            for p in range(n_pairs):
                c = p * LANES
                h0 = 2 * (bi * n_pairs + p)
                vp = qkv_ref[0, r, pl.ds(k0, n_keys), 2 * GROUP_WIDTH + c:2 * GROUP_WIDTH + c + LANES]
                o_lo = _mm(probs[h0], vp) * inv_den[h0]
                o_hi = _mm(probs[h0 + 1], vp) * inv_den[h0 + 1]
                o_ref[0, r, pl.ds(q0, ATT_BLK), c:c + LANES] = jnp.where(low, o_lo, o_hi).astype(o_ref.dtype)
                l_ref[0, r, pl.ds(q0, ATT_BLK), c:c + LANES] = jnp.where(low, lse[h0], lse[h0 + 1])

    def first_block(r):
        return (r, 0, 0, ATT_BLK, first)

    def band_block(r, n):
        q0, k0 = n * ATT_BLK, (n - 1) * ATT_BLK
        if not isinstance(n, int):
            q0, k0 = pl.multiple_of(q0, ATT_BLK), pl.multiple_of(k0, ATT_BLK)
        return (r, q0, k0, 2 * ATT_BLK, band)

    per = BLOCKS_PER_TRIP
    if n_blocks == 1:
        def body(it, carry):
            run([first_block(per * it + i) for i in range(per)])
            return carry
        lax.fori_loop(0, n_streams // per, body, 0)
    else:
        trips = n_blocks // per

        def body(it, carry):
            r = it // trips
            m = it % trips

            @pl.when(m == 0)
            def _():
                run([first_block(r)] + [band_block(r, i) for i in range(1, per)])

            @pl.when(m > 0)
            def _():
                run([band_block(r, per * m + i) for i in range(per)])
            return carry
        lax.fori_loop(0, n_streams * trips, body, 0)


def _attn_prompt(qkv, n_seq, seq, dil):
    ln = seq // dil
    out = jax.ShapeDtypeStruct((n_seq, dil, ln, GROUP_WIDTH), F32)
    ospec = pl.BlockSpec((1, dil, ln, GROUP_WIDTH), lambda b: (b, 0, 0, 0))
    return pl.pallas_call(
        functools.partial(_attn_kernel, n_blocks=ln // ATT_BLK),
        grid=(n_seq,),
        in_specs=[pl.BlockSpec((1, dil, ln, 3 * GROUP_WIDTH), lambda b: (b, 0, 0, 0))],
        out_specs=[ospec, ospec],
        out_shape=[jax.ShapeDtypeStruct(out.shape, BF16), out],
        compiler_params=_params(1),
        name=f"attn_d{dil}",
    )(qkv)


def _combine_groups(os_, ls):
    mx = jnp.maximum(jnp.maximum(ls[0], ls[1]), ls[2])
    es = [jnp.exp(l - mx) for l in ls]
    inv = 1.0 / (es[0] + es[1] + es[2])
    return (es[0] * inv) * os_[0] + (es[1] * inv) * os_[1] + (es[2] * inv) * os_[2]


def _merge_tail(x1, pooled, att, gate, pw_ref, ps_ref, wbp_ref, wba_ref, wo_ref):
    mixed = jnp.concatenate(
        [_mm(pooled[gi].astype(BF16), pw_ref[gi]) for gi in range(len(POOL_WINDOWS))], axis=-1)
    pool_y = (mixed * ps_ref[...]).astype(BF16)
    merged = (gate[:, 0:D_MODEL] * _mm(pool_y, wbp_ref[...])
              + gate[:, D_MODEL:2 * D_MODEL] * _mm(att.astype(BF16), wba_ref[...]))
    return x1 + _mm(merged.astype(BF16), wo_ref[...])


def _window_sums(ext_ref, lvl_ref, gi, w, tm):
    n = tm + POOL_PAD
    cols = slice(gi * POOL_GROUP_WIDTH, (gi + 1) * POOL_GROUP_WIDTH)
    read = lambda lo, hi: ext_ref[lo:hi, cols]
    span, level = 1, gi * POOL_LEVELS
    while 2 * span < w:
        lvl_ref[level, 2 * span:n, :] = read(2 * span, n) + read(span, n - span)
        read = lambda lo, hi, level=level: lvl_ref[level, lo:hi, :]
        span, level = 2 * span, level + 1
    return read(POOL_PAD, POOL_PAD + tm) + read(POOL_PAD - span, POOL_PAD - span + tm)


def _merge_prompt_kernel(x1_ref, u_ref, up_ref, o0_ref, o1_ref, o2_ref, l0_ref, l1_ref, l2_ref,
                         gate_ref, pw_ref, ps_ref, wbp_ref, wba_ref, wo_ref, y_ref,
                         ext_ref, lvl_ref, nat_ref, *, tm):
    j = pl.program_id(1)
    u = u_ref[...]
    ext_ref[0:POOL_PAD, :] = jnp.where(j > 0, up_ref[...], 0.0)
    ext_ref[POOL_PAD:POOL_PAD + tm, :] = u
    pos = j * tm + lax.broadcasted_iota(jnp.int32, (tm, 1), 0)
    pooled = []
    for gi, w in enumerate(POOL_WINDOWS):
        c0 = gi * POOL_GROUP_WIDTH
        cnt = jnp.minimum(pos + 1, w).astype(F32)
        pooled.append(_window_sums(ext_ref, lvl_ref, gi, w, tm) / cnt
                      - u[:, c0:c0 + POOL_GROUP_WIDTH])

    def natural(ref, slot):
        if len(ref.shape) == 2:
            return ref[...].astype(F32)
        d = ref.shape[1]
        n_tiles = GROUP_WIDTH // LANES
        for r in range(d):
            for cl in range(n_tiles):
                nat_ref[slot * n_tiles + cl, pl.ds(r, tm // d, stride=d), :] = (
                    ref[0, r, :, cl * LANES:(cl + 1) * LANES].astype(F32))
        return jnp.concatenate([nat_ref[slot * n_tiles + cl] for cl in range(n_tiles)], axis=-1)

    att = _combine_groups((natural(o0_ref, 0), natural(o1_ref, 1), natural(o2_ref, 2)),
                          (natural(l0_ref, 3), natural(l1_ref, 4), natural(l2_ref, 5)))
    y_ref[...] = _merge_tail(x1_ref[...], pooled, att, gate_ref[...],
                             pw_ref, ps_ref, wbp_ref, wba_ref, wo_ref)


def _merge_prompt(x1, u, os_, ls, gate, pw, ps, wbp, wba, wo, n_seq, seq, tm):
    t = n_seq * seq
    nj = seq // tm
    row = lambda width: pl.BlockSpec((tm, width), lambda b, j: (b * nj + j, 0))
    prev = pl.BlockSpec((POOL_PAD, POOL_WIDTH),
                        lambda b, j: (jnp.maximum((b * nj + j) * (tm // POOL_PAD) - 1, 0), 0))
    att_args, att_specs = [], []
    for a in list(os_) + list(ls):
        d = a.shape[1]
        if d == 1:
            att_args.append(a.reshape(t, GROUP_WIDTH))
            att_specs.append(row(GROUP_WIDTH))
        else:
            att_args.append(a)
            att_specs.append(pl.BlockSpec((1, d, tm // d, GROUP_WIDTH), lambda b, j: (b, 0, j, 0)))
    return pl.pallas_call(
        functools.partial(_merge_prompt_kernel, tm=tm),
        grid=(n_seq, nj),
        in_specs=[row(D_MODEL), row(POOL_WIDTH), prev] + att_specs + [row(2 * D_MODEL)]
        + [_resident(a.shape) for a in (pw, ps, wbp, wba, wo)],
        out_specs=row(D_MODEL),
        out_shape=jax.ShapeDtypeStruct((t, D_MODEL), F32),
        scratch_shapes=[pltpu.VMEM((tm + POOL_PAD, POOL_WIDTH), F32),
                        pltpu.VMEM((len(POOL_WINDOWS) * POOL_LEVELS, tm + POOL_PAD,
                                    POOL_GROUP_WIDTH), F32),
                        pltpu.VMEM((2 * N_GROUPS * GROUP_WIDTH // LANES, tm, LANES), F32)],
        compiler_params=_params(2),
        name="merge_prompt",
    )(x1, u, u, *att_args, gate, pw, ps, wbp, wba, wo)


def _merge_sample_kernel(x1_ref, ctx_ref, att_ref, gate_ref, pw_ref, ps_ref, wbp_ref, wba_ref,
                         wo_ref, y_ref, *, t_new):
    n_seq = ctx_ref.shape[0]
    for t in range(t_new):
        def ctx_row(r, c0):
            return ctx_ref[:, r * POOL_WIDTH + c0:r * POOL_WIDTH + c0 + POOL_GROUP_WIDTH]
        pooled = []
        for gi, w in enumerate(POOL_WINDOWS):
            c0 = gi * POOL_GROUP_WIDTH
            last = POOL_STATE + t
            acc = ctx_row(last, c0)
            for k in range(1, w):
                acc = acc + ctx_row(last - k, c0)
            pooled.append(acc / float(min(w, last + 1)) - ctx_row(last, c0))
        rows = pl.ds(t * n_seq, n_seq)
        y_ref[rows, :] = _merge_tail(x1_ref[rows, :], pooled, att_ref[rows, :], gate_ref[rows, :],
                                     pw_ref, ps_ref, wbp_ref, wba_ref, wo_ref)


def _merge_sample(x1, ctx, att, gate, pw, ps, wbp, wba, wo, t_new):
    args = (x1, ctx, att, gate, pw, ps, wbp, wba, wo)
    return pl.pallas_call(
        functools.partial(_merge_sample_kernel, t_new=t_new),
        grid=(1,),
        in_specs=[pl.BlockSpec(a.shape, lambda i, nd=a.ndim: (0,) * nd) for a in args],
        out_specs=pl.BlockSpec(x1.shape, lambda i: (0, 0)),
        out_shape=jax.ShapeDtypeStruct(x1.shape, F32),
        compiler_params=_params(1),
        name="merge_sample",
    )(*args)


def _cache_to_cols(c):
    n_seq, w = c.shape[0], c.shape[1]
    return c.transpose(0, 2, 3, 4, 1).reshape(n_seq, 2 * GROUP_WIDTH, w)


def _cols_to_cache(ct):
    n_seq, _, w = ct.shape
    return ct.reshape(n_seq, 2, HEADS, HEAD_DIM, w).transpose(0, 4, 1, 2, 3)[None]


def _tile_plan(n_p, seq, n_s, t_new):
    ffn = n_p * seq // n_s
    prompt = 512
    sample = 256
    assert ffn * n_s == n_p * seq and ffn % 8 == 0
    assert seq % prompt == 0 and prompt % (BF16_ROWS * max(d for _, d in ATT_CONFIGS)) == 0
    assert prompt % POOL_PAD == 0
    assert (n_s * t_new) % sample == 0 and LANES % t_new == 0
    return {"ffn": ffn, "prompt": prompt, "sample": sample}


def kernel(x_prompt, x_sample, cache_kv_w128, cache_kv_w512, cache_kv_w2048, state_pool,
           ffn1_norm, ffn1_w_gu, ffn1_w_down, mix_norm, w_in, q_norm, k_norm, pool_w,
           pool_scale, w_branch_pool, w_branch_att, w_out, ffn2_norm, ffn2_w_gu, ffn2_w_down):
    depth = ffn1_norm.shape[0]
    n_p, seq, _ = x_prompt.shape
    n_s, t_new, _ = x_sample.shape
    n_tok = n_s * t_new
    tiles = _tile_plan(n_p, seq, n_s, t_new)
    assert depth == 1
    assert all(c.shape[2] == w for c, (w, _) in
               zip((cache_kv_w128, cache_kv_w512, cache_kv_w2048), ATT_CONFIGS))
    caches_t = [_cache_to_cols(c[0]) for c in (cache_kv_w128, cache_kv_w512, cache_kv_w2048)]
    dils = tuple(d for _, d in ATT_CONFIGS)

    g1 = ffn1_norm[0][None, :]
    g2 = ffn2_norm[0][None, :]
    gm = mix_norm[0][None, :]
    wgu1, wd1 = ffn1_w_gu[0].astype(BF16), ffn1_w_down[0].astype(BF16)
    wgu2, wd2 = ffn2_w_gu[0].astype(BF16), ffn2_w_down[0].astype(BF16)
    win = w_in[0].astype(BF16)
    qn = q_norm[0].reshape(1, ATT_WIDTH)
    kn = k_norm[0].reshape(1, ATT_WIDTH)
    pw = pool_w[0].astype(BF16)
    ps = pool_scale[0][None, :]
    wbp, wba, wo = (w_branch_pool[0].astype(BF16), w_branch_att[0].astype(BF16), w_out[0].astype(BF16))

    xs = x_sample.reshape(n_tok, D_MODEL)
    x1s = _ffn(xs, g1, wgu1, wd1, tm=n_tok)
    us, gates, sq0, sq1, sq2, st0, st1, st2, sk0, sk1, sk2 = _inproj(
        x1s, gm, win, qn, kn, 1, n_tok, tiles["sample"], (n_tok,) * N_GROUPS, (1,) * N_GROUPS,
        row_kv=True)
    new_cols = [a[0] for a in (st0, st1, st2)]
    q_s = jnp.concatenate([a[:, 0:GROUP_WIDTH] for a in (sq0, sq1, sq2)], axis=-1).astype(F32)
    q_s = q_s.reshape(n_s, t_new, ATT_WIDTH)
    new_rows = [a.reshape(n_s, t_new, 2 * GROUP_WIDTH) for a in (sk0, sk1, sk2)]

    xp = x_prompt.reshape(n_p * seq, D_MODEL)
    tm_ffn = tiles["ffn"]
    x1, half_rolled, probs_stats = _ffn(
        xp, g1, wgu1, wd1, tm_ffn, roll=(caches_t, new_cols, 0, t_new, None),
        stage_args=[q_s] + new_rows)
    keeps = tuple(min(w, seq) for w, _ in ATT_CONFIGS)
    u, gate, qkv0, qkv1, qkv2, kvt0, kvt1, kvt2 = _inproj(
        x1, gm, win, qn, kn, n_p, seq, tiles["prompt"], keeps, dils, row_kv=False)
    os_, ls = [], []
    for qkv, dil in zip((qkv0, qkv1, qkv2), dils):
        o, l = _attn_prompt(qkv.reshape(n_p, dil, seq // dil, 3 * GROUP_WIDTH), n_p, seq, dil)
        os_.append(o)
        ls.append(l)
    x2 = _merge_prompt(x1, u, os_, ls, gate, pw, ps, wbp, wba, wo, n_p, seq, tiles["prompt"])
    y_p, new_caches_t, (att_s,) = _ffn(
        x2, g2, wgu2, wd2, tm_ffn, roll=(caches_t, new_cols, 1, t_new, half_rolled),
        stage_args=list(probs_stats) + new_rows)
    y_prompt = y_p.reshape(n_p, seq, D_MODEL)
    new_kv_p = [_cols_to_cache(kvt) for kvt in (kvt0, kvt1, kvt2)]
    new_pool_p = u.reshape(n_p, seq, POOL_WIDTH)[None, :, seq - POOL_STATE:, :]
    new_kv_s = [_cols_to_cache(c) for c in new_caches_t]

    u_ctx = jnp.concatenate([state_pool[0], us.reshape(n_s, t_new, POOL_WIDTH)], axis=1)
    new_pool_s = u_ctx[None, :, t_new:, :]
    to_ts = lambda a: a.reshape(n_s, t_new, -1).transpose(1, 0, 2).reshape(n_tok, -1)
    x2s = _merge_sample(to_ts(x1s), u_ctx.reshape(n_s, (POOL_STATE + t_new) * POOL_WIDTH),
                        to_ts(att_s), to_ts(gates), pw, ps, wbp, wba, wo, t_new)
    y_s = _ffn(x2s, g2, wgu2, wd2, tm=n_tok)
    y_sample = y_s.reshape(t_new, n_s, D_MODEL).transpose(1, 0, 2)

    return (y_prompt, y_sample, new_kv_p[0], new_kv_p[1], new_kv_p[2], new_pool_p,
            new_kv_s[0], new_kv_s[1], new_kv_s[2], new_pool_s)
```

```python
import functools

import jax
import jax.numpy as jnp
from jax import lax
from jax.experimental import pallas as pl
from jax.experimental.pallas import tpu as pltpu

F32 = jnp.float32
BF16 = jnp.bfloat16

D_MODEL = 1024
D_FF = 2816
POOL_WINDOWS = (2, 4, 8, 16)
POOL_GROUP_WIDTH = 128
POOL_WIDTH = 512
POOL_STATE = 15
POOL_PAD = 16
POOL_LEVELS = 3
ATT_CONFIGS = ((128, 1), (512, 4), (2048, 16))
N_GROUPS = 3
HEADS = 8
HEAD_DIM = 64
GROUP_WIDTH = HEADS * HEAD_DIM
ATT_WIDTH = N_GROUPS * GROUP_WIDTH
ATT_BLK = 128
EPS = 1e-6
Q_OFF = POOL_WIDTH
K_OFF = Q_OFF + ATT_WIDTH
V_OFF = K_OFF + ATT_WIDTH
GATE_OFF = V_OFF + ATT_WIDTH
IN_WIDTH = GATE_OFF + 2 * D_MODEL

BLOCKS_PER_TRIP = 2
LANES = 128
BF16_ROWS = 16
FF_CHUNK = 256
VMEM_LIMIT = 56 * 1024 * 1024
VMEM_LIMIT_FULL = 62 * 1024 * 1024


def _params(n_axes, vmem=VMEM_LIMIT):
    return pltpu.CompilerParams(
        dimension_semantics=("arbitrary",) * n_axes, vmem_limit_bytes=vmem)


def _resident(shape):
    nd = len(shape)
    return pl.BlockSpec(shape, lambda *_: (0,) * nd, pipeline_mode=pl.Buffered(1))


def _rmsnorm(x, g):
    ms = jnp.mean(x * x, axis=-1, keepdims=True)
    return x * lax.rsqrt(ms + EPS) * g


def _mm(a, b):
    return jnp.dot(a, b, preferred_element_type=F32)


def _mm_nt(a, b):
    return lax.dot_general(a, b, (((1,), (1,)), ((), ())), preferred_element_type=F32)


def _roll_cache_half(c_ref, kvt_ref, n_ref, seq, t_new):
    w = c_ref.shape[2]
    tail_lane = lax.broadcasted_iota(jnp.int32, (GROUP_WIDTH, LANES), 1) >= LANES - t_new
    shift = (LANES - t_new) - t_new * (seq % (LANES // t_new))
    new_cols = pltpu.roll(kvt_ref[...], shift, axis=1)
    rolled = pltpu.roll(c_ref[0], w - t_new, axis=1)
    if w > LANES:
        n_ref[0, :, 0:w - LANES] = rolled[:, 0:w - LANES]
    n_ref[0, :, w - LANES:w] = jnp.where(tail_lane, new_cols, rolled[:, w - LANES:w])


def _sample_rows(t_new):
    rows = t_new * HEADS
    r_t = lax.broadcasted_iota(jnp.int32, (rows, 1), 0) // HEADS
    r_h = lax.broadcasted_iota(jnp.int32, (rows, GROUP_WIDTH), 0) % HEADS
    diag = (lax.broadcasted_iota(jnp.int32, (rows, GROUP_WIDTH), 1) // HEAD_DIM) == r_h
    return rows, r_t, diag


def _sample_scores(q_ref, kvn_refs, k_refs, p_refs, st_refs, t_new):
    rows, r_t, diag = _sample_rows(t_new)
    lane = lax.broadcasted_iota(jnp.int32, (rows, LANES), 1)
    for g in range(N_GROUPS):
        dil = ATT_CONFIGS[g][1]
        w = k_refs[g].shape[2]
        kn = kvn_refs[g][0][:, 0:GROUP_WIDTH].astype(BF16).astype(F32)
        qg = q_ref[0, :, g * GROUP_WIDTH:(g + 1) * GROUP_WIDTH].astype(BF16).astype(F32)
        qf = jnp.concatenate(
            [jnp.broadcast_to(qg[t:t + 1, :], (HEADS, GROUP_WIDTH)) for t in range(t_new)], axis=0)
        qf = jnp.where(diag, qf, 0.0)
        wi = lax.broadcasted_iota(jnp.int32, (rows, w), 1)
        ok_c = (wi >= r_t) & (((wi - r_t) & (dil - 1)) == 0)
        s_c = _mm(qf.astype(BF16), k_refs[g][0].astype(BF16)) * (HEAD_DIM ** -0.5)
        s_c = jnp.where(ok_c, s_c, -jnp.inf)
        s_n = []
        for t2 in range(t_new):
            ok = (r_t >= t2) & (((r_t - t2) & (dil - 1)) == 0)
            s = jnp.sum(qf * kn[t2:t2 + 1, :], axis=-1, keepdims=True) * (HEAD_DIM ** -0.5)
            s_n.append(jnp.where(ok, s, -jnp.inf))
        mx = jnp.max(s_c, axis=-1, keepdims=True)
        for s in s_n:
            mx = jnp.maximum(mx, s)
        e_c = jnp.exp(s_c - mx)
        e_n = [jnp.exp(s - mx) for s in s_n]
        den = jnp.sum(e_c, axis=-1, keepdims=True)
        for e in e_n:
            den = den + e
        inv = 1.0 / den
        p_refs[g][0] = (e_c * inv).astype(BF16)
        st = jnp.where(lane == t_new, mx + jnp.log(den), 0.0)
        for t2 in range(t_new):
            st = jnp.where(lane == t2, e_n[t2] * inv, st)
        st_refs[g][0] = st


def _sample_values(p_refs, st_refs, kvn_refs, v_refs, att_ref, t_new):
    rows, _, diag = _sample_rows(t_new)
    os_, ls = [], []
    for g in range(N_GROUPS):
        st = st_refs[g][0]
        vn = kvn_refs[g][0][:, GROUP_WIDTH:2 * GROUP_WIDTH].astype(BF16).astype(F32)
        o = _mm_nt(p_refs[g][0], v_refs[g][0].astype(BF16))
        for t2 in range(t_new):
            o = o + st[:, t2:t2 + 1].astype(BF16).astype(F32) * vn[t2:t2 + 1, :]
        os_.append(o)
        ls.append(st[:, t_new:t_new + 1])
    full = jnp.where(diag, _combine_groups(os_, ls), 0.0)
    att_ref[0] = jnp.concatenate(
        [jnp.sum(full[t * HEADS:(t + 1) * HEADS, :], axis=0, keepdims=True) for t in range(t_new)],
        axis=0)


def _ffn_kernel(x_ref, g_ref, wgu_ref, wd_ref, *refs, n_roll, aliased, t_new, stage):
    it = iter(refs)
    take = lambda n: [next(it) for _ in range(n)]
    c_refs, kvt_refs = take(n_roll), take(n_roll)
    take(n_roll if aliased else 0)
    if stage == "scores":
        q_ref, kvn_refs = next(it), take(n_roll)
    elif stage == "values":
        p_refs, st_refs, kvn_refs = take(n_roll), take(n_roll), take(n_roll)
    o_ref = next(it)
    n_refs = take(n_roll)
    if stage == "scores":
        _sample_scores(q_ref, kvn_refs, c_refs, take(n_roll), take(n_roll), t_new)
    elif stage == "values":
        _sample_values(p_refs, st_refs, kvn_refs, c_refs, next(it), t_new)
    acc_ref = next(it)
    for g in range(n_roll):
        _roll_cache_half(c_refs[g], kvt_refs[g], n_refs[g], pl.program_id(0), t_new)
    x = x_ref[...]
    xn = _rmsnorm(x, g_ref[...]).astype(BF16)
    for c, lo in enumerate(range(0, D_FF, FF_CHUNK)):
        hi = min(lo + FF_CHUNK, D_FF)
        a = _mm(xn, wgu_ref[:, lo:hi])
        b = _mm(xn, wgu_ref[:, D_FF + lo:D_FF + hi])
        h = (a * jax.nn.sigmoid(a) * b).astype(BF16)
        d = _mm(h, wd_ref[lo:hi, :])
        if c == 0:
            acc_ref[...] = d
        else:
            acc_ref[...] += d
    o_ref[...] = x + 0.5 * acc_ref[...]


def _ffn(x, g, wgu, wd, tm, roll=None, stage_args=()):
    t = x.shape[0]
    steps = t // tm
    in_specs = [
        pl.BlockSpec((tm, D_MODEL), lambda i: (i, 0)),
        _resident((1, D_MODEL)),
        _resident((D_MODEL, 2 * D_FF)),
        _resident((D_FF, D_MODEL)),
    ]
    args = [x, g, wgu, wd]
    out_specs = [pl.BlockSpec((tm, D_MODEL), lambda i: (i, 0))]
    out_shape = [jax.ShapeDtypeStruct((t, D_MODEL), F32)]
    aliases = {}
    n_roll, t_new, stage = 0, 0, None
    if roll is not None:
        caches_t, kvts, half, t_new, partial = roll
        n_roll = len(caches_t)
        assert all(c.shape[0] == steps for c in caches_t)
        per_tile = LANES // t_new
        half_specs = [pl.BlockSpec((1, GROUP_WIDTH, c.shape[2]), lambda i: (i, half, 0))
                      for c in caches_t]
        in_specs += half_specs
        in_specs += [pl.BlockSpec((GROUP_WIDTH, LANES), lambda i: (half, i // per_tile))] * n_roll
        args += list(caches_t) + list(kvts)
        if partial is not None:
            aliases = {len(args) + k: 1 + k for k in range(n_roll)}
            in_specs += [pl.BlockSpec(memory_space=pl.ANY)] * n_roll
            args += list(partial)
        out_specs += half_specs
        out_shape += [jax.ShapeDtypeStruct(c.shape, F32) for c in caches_t]
        stage = ("scores", "values")[half]
        per_seq = lambda a: pl.BlockSpec((1,) + a.shape[1:], lambda i: (i, 0, 0))
        in_specs += [per_seq(a) for a in stage_args]
        args += list(stage_args)
        n_seq, rows = steps, t_new * HEADS
        if stage == "scores":
            stage_out = ([jax.ShapeDtypeStruct((n_seq, rows, c.shape[2]), BF16) for c in caches_t]
                         + [jax.ShapeDtypeStruct((n_seq, rows, LANES), F32)] * n_roll)
        else:
            stage_out = [jax.ShapeDtypeStruct((n_seq, t_new, GROUP_WIDTH), F32)]
        out_specs += [per_seq(a) for a in stage_out]
        out_shape += stage_out
    outs = pl.pallas_call(
        functools.partial(_ffn_kernel, n_roll=n_roll, aliased=bool(aliases), t_new=t_new,
                          stage=stage),
        grid=(steps,),
        in_specs=in_specs,
        out_specs=out_specs,
        out_shape=out_shape,
        input_output_aliases=aliases,
        scratch_shapes=[pltpu.VMEM((tm, D_MODEL), F32)],
        compiler_params=_params(1, vmem=VMEM_LIMIT if roll is None else VMEM_LIMIT_FULL),
        name="ffn",
    )(*args)
    return outs[0] if roll is None else (outs[0], outs[1:1 + n_roll], outs[1 + n_roll:])


def _head_rmsnorm(x, g):
    low = lax.broadcasted_iota(jnp.int32, (1, LANES), 1) < HEAD_DIM
    outs = []
    for p in range(GROUP_WIDTH // LANES):
        xp = x[:, p * LANES:(p + 1) * LANES]
        sq = xp * xp
        s_all = jnp.sum(sq, axis=-1, keepdims=True)
        s_low = jnp.sum(jnp.where(low, sq, 0.0), axis=-1, keepdims=True)
        ms = jnp.where(low, s_low, s_all - s_low) * (1.0 / HEAD_DIM)
        outs.append(xp * lax.rsqrt(ms + EPS))
    return jnp.concatenate(outs, axis=-1) * g


def _in_kernel(x_ref, g_ref, w_ref, qn_ref, kn_ref, *refs, tm, dils, kv_from, kv_rows, row_kv):
    u_ref, gate_ref = refs[0:2]
    qkv_refs = refs[2:5]
    kvt_refs = refs[5:8]
    kvrow_refs = refs[8:11] if row_kv else None
    stage_ref, keep_ref = refs[-2:]
    j = pl.program_id(1)
    h = _rmsnorm(x_ref[...], g_ref[...]).astype(BF16)
    gate_ref[...] = jax.nn.sigmoid(_mm(h, w_ref[:, GATE_OFF:IN_WIDTH])).astype(gate_ref.dtype)
    for g in range(N_GROUPS):
        c0, c1 = g * GROUP_WIDTH, (g + 1) * GROUP_WIDTH
        q = _head_rmsnorm(_mm(h, w_ref[:, Q_OFF + c0:Q_OFF + c1]), qn_ref[:, c0:c1])
        k = _head_rmsnorm(_mm(h, w_ref[:, K_OFF + c0:K_OFF + c1]), kn_ref[:, c0:c1])
        v = _mm(h, w_ref[:, V_OFF + c0:V_OFF + c1])
        d = dils[g]
        if d == 1:
            qkv_refs[g][:, 0:GROUP_WIDTH] = q.astype(BF16)
            qkv_refs[g][:, GROUP_WIDTH:2 * GROUP_WIDTH] = k.astype(BF16)
            qkv_refs[g][:, 2 * GROUP_WIDTH:3 * GROUP_WIDTH] = v.astype(BF16)
        else:
            for ci, a in enumerate((q, k, v)):
                for cl in range(GROUP_WIDTH // LANES):
                    stage_ref[ci * (GROUP_WIDTH // LANES) + cl] = a[:, cl * LANES:(cl + 1) * LANES]
            for r in range(d):
                for cl in range(3 * GROUP_WIDTH // LANES):
                    qkv_refs[g][0, r, :, cl * LANES:(cl + 1) * LANES] = (
                        stage_ref[cl, pl.ds(r, tm // d, stride=d), :].astype(BF16))
        if row_kv:
            kvrow_refs[g][:, 0:GROUP_WIDTH] = k
            kvrow_refs[g][:, GROUP_WIDTH:2 * GROUP_WIDTH] = v

        r0 = kv_rows[g]
        if kv_from[g] == 0:
            kvt_refs[g][0, 0:GROUP_WIDTH, :] = k[r0:, :].T
            kvt_refs[g][0, GROUP_WIDTH:2 * GROUP_WIDTH, :] = v[r0:, :].T
        else:
            keep_ref[g, 0, 0:tm - r0, :] = k[r0:, :]
            keep_ref[g, 1, 0:tm - r0, :] = v[r0:, :]
    u_ref[...] = _mm(h, w_ref[:, 0:POOL_WIDTH])
    for g in range(N_GROUPS):
        if kv_from[g] > 0:
            @pl.when(j >= kv_from[g])
            def _(g=g):
                rows = tm - kv_rows[g]
                kvt_refs[g][0, 0:GROUP_WIDTH, :] = keep_ref[g, 0, 0:rows, :].T
                kvt_refs[g][0, GROUP_WIDTH:2 * GROUP_WIDTH, :] = keep_ref[g, 1, 0:rows, :].T


def _inproj(x1, g, w_in, qn, kn, n_seq, seq, tm, keeps, dils, row_kv):
    t = n_seq * seq
    nj = seq // tm
    row = lambda width: pl.BlockSpec((tm, width), lambda b, j: (b * nj + j, 0))
    qkv_specs, qkv_shapes = [], []
    for d in dils:
        if d == 1:
            qkv_specs.append(row(3 * GROUP_WIDTH))
            qkv_shapes.append(jax.ShapeDtypeStruct((t, 3 * GROUP_WIDTH), BF16))
        else:
            qkv_specs.append(pl.BlockSpec((1, d, tm // d, 3 * GROUP_WIDTH), lambda b, j: (b, 0, j, 0)))
            qkv_shapes.append(jax.ShapeDtypeStruct((n_seq, d, seq // d, 3 * GROUP_WIDTH), BF16))
    kv_from, kv_rows, kvt_specs, kvt_shapes = [], [], [], []
    for keep in keeps:
        tw = min(tm, keep)
        nb = keep // tw
        kv_from.append(nj - nb)
        kv_rows.append(tm - tw)
        kvt_specs.append(pl.BlockSpec(
            (1, 2 * GROUP_WIDTH, tw), lambda b, j, nb=nb: (b, 0, jnp.maximum(j - (nj - nb), 0))))
        kvt_shapes.append(jax.ShapeDtypeStruct((n_seq, 2 * GROUP_WIDTH, keep), F32))
    keep_rows = max([tm - r for r, f in zip(kv_rows, kv_from) if f > 0], default=8)
    out_specs = [row(POOL_WIDTH), row(2 * D_MODEL)] + qkv_specs + kvt_specs
    out_shape = [jax.ShapeDtypeStruct((t, POOL_WIDTH), F32),
                 jax.ShapeDtypeStruct((t, 2 * D_MODEL), BF16)] + qkv_shapes + kvt_shapes
    if row_kv:
        out_specs += [row(2 * GROUP_WIDTH)] * N_GROUPS
        out_shape += [jax.ShapeDtypeStruct((t, 2 * GROUP_WIDTH), F32)] * N_GROUPS
    return pl.pallas_call(
        functools.partial(_in_kernel, tm=tm, dils=tuple(dils), kv_from=tuple(kv_from),
                          kv_rows=tuple(kv_rows), row_kv=row_kv),
        grid=(n_seq, nj),
        in_specs=[
            row(D_MODEL),
            _resident((1, D_MODEL)),
            _resident((D_MODEL, IN_WIDTH)),
            _resident((1, ATT_WIDTH)),
            _resident((1, ATT_WIDTH)),
        ],
        out_specs=out_specs,
        out_shape=out_shape,
        scratch_shapes=[pltpu.VMEM((3 * GROUP_WIDTH // LANES, tm, LANES), F32),
                        pltpu.VMEM((N_GROUPS, 2, keep_rows, GROUP_WIDTH), F32)],
        compiler_params=_params(2, vmem=VMEM_LIMIT_FULL),
        name="inproj",
    )(x1, g, w_in, qn, kn)


def _attn_kernel(qkv_ref, o_ref, l_ref, *, n_blocks):
    n_streams = qkv_ref.shape[1]
    i = lax.broadcasted_iota(jnp.int32, (ATT_BLK, 2 * ATT_BLK), 0)
    j = lax.broadcasted_iota(jnp.int32, (ATT_BLK, 2 * ATT_BLK), 1)
    band = (j >= i) & (j <= i + ATT_BLK)
    first = band[:, ATT_BLK:]
    low = lax.broadcasted_iota(jnp.int32, (ATT_BLK, LANES), 1) < HEAD_DIM
    low_keys = {n: lax.broadcasted_iota(jnp.int32, (n, LANES), 1) < HEAD_DIM
                for n in (ATT_BLK, 2 * ATT_BLK)}
    scale = jnp.asarray(HEAD_DIM ** -0.5, BF16)

    def run(blocks):
        n_pairs = GROUP_WIDTH // LANES
        scores = []
        for r, q0, k0, n_keys, mask in blocks:
            for p in range(n_pairs):
                c = p * LANES
                qp = qkv_ref[0, r, pl.ds(q0, ATT_BLK), c:c + LANES] * scale
                kp = qkv_ref[0, r, pl.ds(k0, n_keys), GROUP_WIDTH + c:GROUP_WIDTH + c + LANES]
                for own in (low, ~low):
                    s = _mm_nt(jnp.where(own, qp, jnp.zeros_like(qp)), kp)
                    scores.append(jnp.where(mask, s, -jnp.inf))
        probs, mxs = [], []
        for s in scores:
            mx = jnp.max(s, axis=-1, keepdims=True)
            probs.append(jnp.exp((s - mx).astype(BF16)))
            mxs.append(mx)
        for bi, (r, q0, k0, n_keys, _) in enumerate(blocks):
            for p in range(n_pairs):
                c = p * LANES
                h0 = 2 * (bi * n_pairs + p)
                vp = qkv_ref[0, r, pl.ds(k0, n_keys), 2 * GROUP_WIDTH + c:2 * GROUP_WIDTH + c + LANES]
                own_k = low_keys[n_keys]
                one = jnp.ones_like(vp)
                t_lo = _mm(probs[h0], jnp.where(own_k, vp, one))
                t_hi = _mm(probs[h0 + 1], jnp.where(own_k, one, vp))
                den = pltpu.roll(jnp.where(low, t_hi, t_lo), HEAD_DIM, axis=1)
                o = jnp.where(low, t_lo, t_hi) * (1.0 / den)
                o_ref[0, r, pl.ds(q0, ATT_BLK), c:c + LANES] = o.astype(o_ref.dtype)
                l_ref[0, r, pl.ds(q0, ATT_BLK), c:c + LANES] = (
                    jnp.where(low, mxs[h0], mxs[h0 + 1]) + jnp.log(den))

    def first_block(r):
        return (r, 0, 0, ATT_BLK, first)

    def band_block(r, n):
        q0, k0 = n * ATT_BLK, (n - 1) * ATT_BLK
        if not isinstance(n, int):
            q0, k0 = pl.multiple_of(q0, ATT_BLK), pl.multiple_of(k0, ATT_BLK)
        return (r, q0, k0, 2 * ATT_BLK, band)

    per = BLOCKS_PER_TRIP
    if n_blocks == 1:
        def body(it, carry):
            run([first_block(per * it + i) for i in range(per)])
            return carry
        lax.fori_loop(0, n_streams // per, body, 0)
    else:
        trips = n_blocks // per

        def body(it, carry):
            r = it // trips
            m = it % trips

            @pl.when(m == 0)
            def _():
                run([first_block(r)] + [band_block(r, i) for i in range(1, per)])

            @pl.when(m > 0)
            def _():
                run([band_block(r, per * m + i) for i in range(per)])
            return carry
        lax.fori_loop(0, n_streams * trips, body, 0)


def _attn_prompt(qkv, n_seq, seq, dil):
    ln = seq // dil
    out = jax.ShapeDtypeStruct((n_seq, dil, ln, GROUP_WIDTH), F32)
    ospec = pl.BlockSpec((1, dil, ln, GROUP_WIDTH), lambda b: (b, 0, 0, 0))
    return pl.pallas_call(
        functools.partial(_attn_kernel, n_blocks=ln // ATT_BLK),
        grid=(n_seq,),
        in_specs=[pl.BlockSpec((1, dil, ln, 3 * GROUP_WIDTH), lambda b: (b, 0, 0, 0))],
        out_specs=[ospec, ospec],
        out_shape=[jax.ShapeDtypeStruct(out.shape, BF16), out],
        compiler_params=_params(1),
        name=f"attn_d{dil}",
    )(qkv)


def _combine_groups(os_, ls):
    mx = jnp.maximum(jnp.maximum(ls[0], ls[1]), ls[2])
    es = [jnp.exp(l - mx) for l in ls]
    inv = 1.0 / (es[0] + es[1] + es[2])
    return (es[0] * inv) * os_[0] + (es[1] * inv) * os_[1] + (es[2] * inv) * os_[2]


def _merge_tail(x1, pooled, att, gate, pw_ref, ps_ref, wbp_ref, wba_ref, wo_ref):
    mixed = jnp.concatenate(
        [_mm(pooled[gi].astype(BF16), pw_ref[gi]) for gi in range(len(POOL_WINDOWS))], axis=-1)
    pool_y = (mixed * ps_ref[...]).astype(BF16)
    merged = (gate[:, 0:D_MODEL] * _mm(pool_y, wbp_ref[...])
              + gate[:, D_MODEL:2 * D_MODEL] * _mm(att.astype(BF16), wba_ref[...]))
    return x1 + _mm(merged.astype(BF16), wo_ref[...])


def _window_sums(ext_ref, lvl_ref, gi, w, tm):
    n = tm + POOL_PAD
    cols = slice(gi * POOL_GROUP_WIDTH, (gi + 1) * POOL_GROUP_WIDTH)
    read = lambda lo, hi: ext_ref[lo:hi, cols]
    span, level = 1, gi * POOL_LEVELS
    while 2 * span < w:
        lvl_ref[level, 2 * span:n, :] = read(2 * span, n) + read(span, n - span)
        read = lambda lo, hi, level=level: lvl_ref[level, lo:hi, :]
        span, level = 2 * span, level + 1
    return read(POOL_PAD, POOL_PAD + tm) + read(POOL_PAD - span, POOL_PAD - span + tm)


def _merge_prompt_kernel(x1_ref, u_ref, up_ref, o0_ref, o1_ref, o2_ref, l0_ref, l1_ref, l2_ref,
                         gate_ref, pw_ref, ps_ref, wbp_ref, wba_ref, wo_ref, y_ref,
                         ext_ref, lvl_ref, nat_ref, *, tm):
    j = pl.program_id(1)
    u = u_ref[...]
    ext_ref[0:POOL_PAD, :] = jnp.where(j > 0, up_ref[...], 0.0)
    ext_ref[POOL_PAD:POOL_PAD + tm, :] = u
    pos = j * tm + lax.broadcasted_iota(jnp.int32, (tm, 1), 0)
    pooled = []
    for gi, w in enumerate(POOL_WINDOWS):
        c0 = gi * POOL_GROUP_WIDTH
        cnt = jnp.minimum(pos + 1, w).astype(F32)
        pooled.append(_window_sums(ext_ref, lvl_ref, gi, w, tm) / cnt
                      - u[:, c0:c0 + POOL_GROUP_WIDTH])

    def natural(ref, slot):
        if len(ref.shape) == 2:
            return ref[...].astype(F32)
        d = ref.shape[1]
        n_tiles = GROUP_WIDTH // LANES
        for r in range(d):
            for cl in range(n_tiles):
                nat_ref[slot * n_tiles + cl, pl.ds(r, tm // d, stride=d), :] = (
                    ref[0, r, :, cl * LANES:(cl + 1) * LANES].astype(F32))
        return jnp.concatenate([nat_ref[slot * n_tiles + cl] for cl in range(n_tiles)], axis=-1)

    att = _combine_groups((natural(o0_ref, 0), natural(o1_ref, 1), natural(o2_ref, 2)),
                          (natural(l0_ref, 3), natural(l1_ref, 4), natural(l2_ref, 5)))
    y_ref[...] = _merge_tail(x1_ref[...], pooled, att, gate_ref[...],
                             pw_ref, ps_ref, wbp_ref, wba_ref, wo_ref)


def _merge_prompt(x1, u, os_, ls, gate, pw, ps, wbp, wba, wo, n_seq, seq, tm):
    t = n_seq * seq
    nj = seq // tm
    row = lambda width: pl.BlockSpec((tm, width), lambda b, j: (b * nj + j, 0))
    prev = pl.BlockSpec((POOL_PAD, POOL_WIDTH),
                        lambda b, j: (jnp.maximum((b * nj + j) * (tm // POOL_PAD) - 1, 0), 0))
    att_args, att_specs = [], []
    for a in list(os_) + list(ls):
        d = a.shape[1]
        if d == 1:
            att_args.append(a.reshape(t, GROUP_WIDTH))
            att_specs.append(row(GROUP_WIDTH))
        else:
            att_args.append(a)
            att_specs.append(pl.BlockSpec((1, d, tm // d, GROUP_WIDTH), lambda b, j: (b, 0, j, 0)))
    return pl.pallas_call(
        functools.partial(_merge_prompt_kernel, tm=tm),
        grid=(n_seq, nj),
        in_specs=[row(D_MODEL), row(POOL_WIDTH), prev] + att_specs + [row(2 * D_MODEL)]
        + [_resident(a.shape) for a in (pw, ps, wbp, wba, wo)],
        out_specs=row(D_MODEL),
        out_shape=jax.ShapeDtypeStruct((t, D_MODEL), F32),
        scratch_shapes=[pltpu.VMEM((tm + POOL_PAD, POOL_WIDTH), F32),
                        pltpu.VMEM((len(POOL_WINDOWS) * POOL_LEVELS, tm + POOL_PAD,
                                    POOL_GROUP_WIDTH), F32),
                        pltpu.VMEM((2 * N_GROUPS * GROUP_WIDTH // LANES, tm, LANES), F32)],
        compiler_params=_params(2),
        name="merge_prompt",
    )(x1, u, u, *att_args, gate, pw, ps, wbp, wba, wo)


def _merge_sample_kernel(x1_ref, ctx_ref, att_ref, gate_ref, pw_ref, ps_ref, wbp_ref, wba_ref,
                         wo_ref, y_ref, *, t_new):
    n_seq = ctx_ref.shape[0]
    for t in range(t_new):
        def ctx_row(r, c0):
            return ctx_ref[:, r * POOL_WIDTH + c0:r * POOL_WIDTH + c0 + POOL_GROUP_WIDTH]
        pooled = []
        for gi, w in enumerate(POOL_WINDOWS):
            c0 = gi * POOL_GROUP_WIDTH
            last = POOL_STATE + t
            acc = ctx_row(last, c0)
            for k in range(1, w):
                acc = acc + ctx_row(last - k, c0)
            pooled.append(acc / float(min(w, last + 1)) - ctx_row(last, c0))
        rows = pl.ds(t * n_seq, n_seq)
        y_ref[rows, :] = _merge_tail(x1_ref[rows, :], pooled, att_ref[rows, :], gate_ref[rows, :],
                                     pw_ref, ps_ref, wbp_ref, wba_ref, wo_ref)


def _merge_sample(x1, ctx, att, gate, pw, ps, wbp, wba, wo, t_new):
    args = (x1, ctx, att, gate, pw, ps, wbp, wba, wo)
    return pl.pallas_call(
        functools.partial(_merge_sample_kernel, t_new=t_new),
        grid=(1,),
        in_specs=[pl.BlockSpec(a.shape, lambda i, nd=a.ndim: (0,) * nd) for a in args],
        out_specs=pl.BlockSpec(x1.shape, lambda i: (0, 0)),
        out_shape=jax.ShapeDtypeStruct(x1.shape, F32),
        compiler_params=_params(1),
        name="merge_sample",
    )(*args)


def _cache_to_cols(c):
    n_seq, w = c.shape[0], c.shape[1]
    return c.transpose(0, 2, 3, 4, 1).reshape(n_seq, 2 * GROUP_WIDTH, w)


def _cols_to_cache(ct):
    n_seq, _, w = ct.shape
    return ct.reshape(n_seq, 2, HEADS, HEAD_DIM, w).transpose(0, 4, 1, 2, 3)[None]


def _tile_plan(n_p, seq, n_s, t_new):
    ffn = n_p * seq // n_s
    prompt = 512
    sample = 256
    assert ffn * n_s == n_p * seq and ffn % 8 == 0
    assert seq % prompt == 0 and prompt % (BF16_ROWS * max(d for _, d in ATT_CONFIGS)) == 0
    assert prompt % POOL_PAD == 0
    assert (n_s * t_new) % sample == 0 and LANES % t_new == 0
    return {"ffn": ffn, "prompt": prompt, "sample": sample}


def kernel(x_prompt, x_sample, cache_kv_w128, cache_kv_w512, cache_kv_w2048, state_pool,
           ffn1_norm, ffn1_w_gu, ffn1_w_down, mix_norm, w_in, q_norm, k_norm, pool_w,
           pool_scale, w_branch_pool, w_branch_att, w_out, ffn2_norm, ffn2_w_gu, ffn2_w_down):
    depth = ffn1_norm.shape[0]
    n_p, seq, _ = x_prompt.shape
    n_s, t_new, _ = x_sample.shape
    n_tok = n_s * t_new
    tiles = _tile_plan(n_p, seq, n_s, t_new)
    assert depth == 1
    assert all(c.shape[2] == w for c, (w, _) in
               zip((cache_kv_w128, cache_kv_w512, cache_kv_w2048), ATT_CONFIGS))
    caches_t = [_cache_to_cols(c[0]) for c in (cache_kv_w128, cache_kv_w512, cache_kv_w2048)]
    dils = tuple(d for _, d in ATT_CONFIGS)

    g1 = ffn1_norm[0][None, :]
    g2 = ffn2_norm[0][None, :]
    gm = mix_norm[0][None, :]
    wgu1, wd1 = ffn1_w_gu[0].astype(BF16), ffn1_w_down[0].astype(BF16)
    wgu2, wd2 = ffn2_w_gu[0].astype(BF16), ffn2_w_down[0].astype(BF16)
    win = w_in[0].astype(BF16)
    qn = q_norm[0].reshape(1, ATT_WIDTH)
    kn = k_norm[0].reshape(1, ATT_WIDTH)
    pw = pool_w[0].astype(BF16)
    ps = pool_scale[0][None, :]
    wbp, wba, wo = (w_branch_pool[0].astype(BF16), w_branch_att[0].astype(BF16), w_out[0].astype(BF16))

    xs = x_sample.reshape(n_tok, D_MODEL)
    x1s = _ffn(xs, g1, wgu1, wd1, tm=n_tok)
    us, gates, sq0, sq1, sq2, st0, st1, st2, sk0, sk1, sk2 = _inproj(
        x1s, gm, win, qn, kn, 1, n_tok, tiles["sample"], (n_tok,) * N_GROUPS, (1,) * N_GROUPS,
        row_kv=True)
    new_cols = [a[0] for a in (st0, st1, st2)]
    q_s = jnp.concatenate([a[:, 0:GROUP_WIDTH] for a in (sq0, sq1, sq2)], axis=-1).astype(F32)
    q_s = q_s.reshape(n_s, t_new, ATT_WIDTH)
    new_rows = [a.reshape(n_s, t_new, 2 * GROUP_WIDTH) for a in (sk0, sk1, sk2)]

    xp = x_prompt.reshape(n_p * seq, D_MODEL)
    tm_ffn = tiles["ffn"]
    x1, half_rolled, probs_stats = _ffn(
        xp, g1, wgu1, wd1, tm_ffn, roll=(caches_t, new_cols, 0, t_new, None),
        stage_args=[q_s] + new_rows)
    keeps = tuple(min(w, seq) for w, _ in ATT_CONFIGS)
    u, gate, qkv0, qkv1, qkv2, kvt0, kvt1, kvt2 = _inproj(
        x1, gm, win, qn, kn, n_p, seq, tiles["prompt"], keeps, dils, row_kv=False)
    os_, ls = [], []
    for qkv, dil in zip((qkv0, qkv1, qkv2), dils):
        o, l = _attn_prompt(qkv.reshape(n_p, dil, seq // dil, 3 * GROUP_WIDTH), n_p, seq, dil)
        os_.append(o)
        ls.append(l)
    x2 = _merge_prompt(x1, u, os_, ls, gate, pw, ps, wbp, wba, wo, n_p, seq, tiles["prompt"])
    y_p, new_caches_t, (att_s,) = _ffn(
        x2, g2, wgu2, wd2, tm_ffn, roll=(caches_t, new_cols, 1, t_new, half_rolled),
        stage_args=list(probs_stats) + new_rows)
    y_prompt = y_p.reshape(n_p, seq, D_MODEL)
    new_kv_p = [_cols_to_cache(kvt) for kvt in (kvt0, kvt1, kvt2)]
    new_pool_p = u.reshape(n_p, seq, POOL_WIDTH)[None, :, seq - POOL_STATE:, :]
    new_kv_s = [_cols_to_cache(c) for c in new_caches_t]

    u_ctx = jnp.concatenate([state_pool[0], us.reshape(n_s, t_new, POOL_WIDTH)], axis=1)
    new_pool_s = u_ctx[None, :, t_new:, :]
    to_ts = lambda a: a.reshape(n_s, t_new, -1).transpose(1, 0, 2).reshape(n_tok, -1)
    x2s = _merge_sample(to_ts(x1s), u_ctx.reshape(n_s, (POOL_STATE + t_new) * POOL_WIDTH),
                        to_ts(att_s), to_ts(gates), pw, ps, wbp, wba, wo, t_new)
    y_s = _ffn(x2s, g2, wgu2, wd2, tm=n_tok)
    y_sample = y_s.reshape(t_new, n_s, D_MODEL).transpose(1, 0, 2)

    return (y_prompt, y_sample, new_kv_p[0], new_kv_p[1], new_kv_p[2], new_pool_p,
            new_kv_s[0], new_kv_s[1], new_kv_s[2], new_pool_s)
```

```python
import functools

import jax
import jax.numpy as jnp
from jax import lax
from jax.experimental import pallas as pl
from jax.experimental.pallas import tpu as pltpu

F32 = jnp.float32
BF16 = jnp.bfloat16

D_MODEL = 1024
D_FF = 2816
POOL_WINDOWS = (2, 4, 8, 16)
POOL_GROUP_WIDTH = 128
POOL_WIDTH = 512
POOL_STATE = 15
POOL_PAD = 16
POOL_LEVELS = 3
ATT_CONFIGS = ((128, 1), (512, 4), (2048, 16))
N_GROUPS = 3
HEADS = 8
HEAD_DIM = 64
GROUP_WIDTH = HEADS * HEAD_DIM
ATT_WIDTH = N_GROUPS * GROUP_WIDTH
ATT_BLK = 128
EPS = 1e-6
Q_OFF = POOL_WIDTH
K_OFF = Q_OFF + ATT_WIDTH
V_OFF = K_OFF + ATT_WIDTH
GATE_OFF = V_OFF + ATT_WIDTH
IN_WIDTH = GATE_OFF + 2 * D_MODEL

BLOCKS_PER_TRIP = 4
LANES = 128
BF16_ROWS = 16
FF_CHUNK = 256
VMEM_LIMIT = 56 * 1024 * 1024
VMEM_LIMIT_FULL = 62 * 1024 * 1024


def _params(n_axes, vmem=VMEM_LIMIT):
    return pltpu.CompilerParams(
        dimension_semantics=("arbitrary",) * n_axes, vmem_limit_bytes=vmem)


def _resident(shape):
    nd = len(shape)
    return pl.BlockSpec(shape, lambda *_: (0,) * nd, pipeline_mode=pl.Buffered(1))


def _rmsnorm(x, g):
    ms = jnp.mean(x * x, axis=-1, keepdims=True)
    return x * lax.rsqrt(ms + EPS) * g


def _mm(a, b):
    return jnp.dot(a, b, preferred_element_type=F32)


def _mm_nt(a, b):
    return lax.dot_general(a, b, (((1,), (1,)), ((), ())), preferred_element_type=F32)


def _roll_cache_half(c_ref, kvt_ref, n_ref, seq, t_new):
    w = c_ref.shape[2]
    tail_lane = lax.broadcasted_iota(jnp.int32, (GROUP_WIDTH, LANES), 1) >= LANES - t_new
    shift = (LANES - t_new) - t_new * (seq % (LANES // t_new))
    new_cols = pltpu.roll(kvt_ref[...], shift, axis=1)
    rolled = pltpu.roll(c_ref[0], w - t_new, axis=1)
    if w > LANES:
        n_ref[0, :, 0:w - LANES] = rolled[:, 0:w - LANES]
    n_ref[0, :, w - LANES:w] = jnp.where(tail_lane, new_cols, rolled[:, w - LANES:w])


def _sample_rows(t_new):
    rows = t_new * HEADS
    r_t = lax.broadcasted_iota(jnp.int32, (rows, 1), 0) // HEADS
    r_h = lax.broadcasted_iota(jnp.int32, (rows, GROUP_WIDTH), 0) % HEADS
    diag = (lax.broadcasted_iota(jnp.int32, (rows, GROUP_WIDTH), 1) // HEAD_DIM) == r_h
    return rows, r_t, diag


def _sample_scores(q_ref, kvn_refs, k_refs, p_refs, st_refs, t_new):
    rows, r_t, diag = _sample_rows(t_new)
    lane = lax.broadcasted_iota(jnp.int32, (rows, LANES), 1)
    for g in range(N_GROUPS):
        dil = ATT_CONFIGS[g][1]
        w = k_refs[g].shape[2]
        kn = kvn_refs[g][0][:, 0:GROUP_WIDTH].astype(BF16).astype(F32)
        qg = q_ref[0, :, g * GROUP_WIDTH:(g + 1) * GROUP_WIDTH].astype(BF16).astype(F32)
        qf = jnp.concatenate(
            [jnp.broadcast_to(qg[t:t + 1, :], (HEADS, GROUP_WIDTH)) for t in range(t_new)], axis=0)
        qf = jnp.where(diag, qf, 0.0)
        wi = lax.broadcasted_iota(jnp.int32, (rows, w), 1)
        ok_c = (wi >= r_t) & (((wi - r_t) & (dil - 1)) == 0)
        s_c = _mm(qf.astype(BF16), k_refs[g][0].astype(BF16)) * (HEAD_DIM ** -0.5)
        s_c = jnp.where(ok_c, s_c, -jnp.inf)
        s_n = []
        for t2 in range(t_new):
            ok = (r_t >= t2) & (((r_t - t2) & (dil - 1)) == 0)
            s = jnp.sum(qf * kn[t2:t2 + 1, :], axis=-1, keepdims=True) * (HEAD_DIM ** -0.5)
            s_n.append(jnp.where(ok, s, -jnp.inf))
        mx = jnp.max(s_c, axis=-1, keepdims=True)
        for s in s_n:
            mx = jnp.maximum(mx, s)
        e_c = jnp.exp(s_c - mx)
        e_n = [jnp.exp(s - mx) for s in s_n]
        den = jnp.sum(e_c, axis=-1, keepdims=True)
        for e in e_n:
            den = den + e
        inv = 1.0 / den
        p_refs[g][0] = (e_c * inv).astype(BF16)
        st = jnp.where(lane == t_new, mx + jnp.log(den), 0.0)
        for t2 in range(t_new):
            st = jnp.where(lane == t2, e_n[t2] * inv, st)
        st_refs[g][0] = st


def _sample_values(p_refs, st_refs, kvn_refs, v_refs, att_ref, t_new):
    rows, _, diag = _sample_rows(t_new)
    os_, ls = [], []
    for g in range(N_GROUPS):
        st = st_refs[g][0]
        vn = kvn_refs[g][0][:, GROUP_WIDTH:2 * GROUP_WIDTH].astype(BF16).astype(F32)
        o = _mm_nt(p_refs[g][0], v_refs[g][0].astype(BF16))
        for t2 in range(t_new):
            o = o + st[:, t2:t2 + 1].astype(BF16).astype(F32) * vn[t2:t2 + 1, :]
        os_.append(o)
        ls.append(st[:, t_new:t_new + 1])
    full = jnp.where(diag, _combine_groups(os_, ls), 0.0)
    att_ref[0] = jnp.concatenate(
        [jnp.sum(full[t * HEADS:(t + 1) * HEADS, :], axis=0, keepdims=True) for t in range(t_new)],
        axis=0)


def _ffn_kernel(x_ref, g_ref, wgu_ref, wd_ref, *refs, n_roll, aliased, t_new, stage):
    it = iter(refs)
    take = lambda n: [next(it) for _ in range(n)]
    c_refs, kvt_refs = take(n_roll), take(n_roll)
    take(n_roll if aliased else 0)
    if stage == "scores":
        q_ref, kvn_refs = next(it), take(n_roll)
    elif stage == "values":
        p_refs, st_refs, kvn_refs = take(n_roll), take(n_roll), take(n_roll)
    o_ref = next(it)
    n_refs = take(n_roll)
    if stage == "scores":
        _sample_scores(q_ref, kvn_refs, c_refs, take(n_roll), take(n_roll), t_new)
    elif stage == "values":
        _sample_values(p_refs, st_refs, kvn_refs, c_refs, next(it), t_new)
    acc_ref = next(it)
    for g in range(n_roll):
        _roll_cache_half(c_refs[g], kvt_refs[g], n_refs[g], pl.program_id(0), t_new)
    x = x_ref[...]
    xn = _rmsnorm(x, g_ref[...]).astype(BF16)
    for c, lo in enumerate(range(0, D_FF, FF_CHUNK)):
        hi = min(lo + FF_CHUNK, D_FF)
        a = _mm(xn, wgu_ref[:, lo:hi])
        b = _mm(xn, wgu_ref[:, D_FF + lo:D_FF + hi])
        h = (a * jax.nn.sigmoid(a) * b).astype(BF16)
        d = _mm(h, wd_ref[lo:hi, :])
        if c == 0:
            acc_ref[...] = d
        else:
            acc_ref[...] += d
    o_ref[...] = x + 0.5 * acc_ref[...]


def _ffn(x, g, wgu, wd, tm, roll=None, stage_args=()):
    t = x.shape[0]
    steps = t // tm
    in_specs = [
        pl.BlockSpec((tm, D_MODEL), lambda i: (i, 0)),
        _resident((1, D_MODEL)),
        _resident((D_MODEL, 2 * D_FF)),
        _resident((D_FF, D_MODEL)),
    ]
    args = [x, g, wgu, wd]
    out_specs = [pl.BlockSpec((tm, D_MODEL), lambda i: (i, 0))]
    out_shape = [jax.ShapeDtypeStruct((t, D_MODEL), F32)]
    aliases = {}
    n_roll, t_new, stage = 0, 0, None
    if roll is not None:
        caches_t, kvts, half, t_new, partial = roll
        n_roll = len(caches_t)
        assert all(c.shape[0] == steps for c in caches_t)
        per_tile = LANES // t_new
        half_specs = [pl.BlockSpec((1, GROUP_WIDTH, c.shape[2]), lambda i: (i, half, 0))
                      for c in caches_t]
        in_specs += half_specs
        in_specs += [pl.BlockSpec((GROUP_WIDTH, LANES), lambda i: (half, i // per_tile))] * n_roll
        args += list(caches_t) + list(kvts)
        if partial is not None:
            aliases = {len(args) + k: 1 + k for k in range(n_roll)}
            in_specs += [pl.BlockSpec(memory_space=pl.ANY)] * n_roll
            args += list(partial)
        out_specs += half_specs
        out_shape += [jax.ShapeDtypeStruct(c.shape, F32) for c in caches_t]
        stage = ("scores", "values")[half]
        per_seq = lambda a: pl.BlockSpec((1,) + a.shape[1:], lambda i: (i, 0, 0))
        in_specs += [per_seq(a) for a in stage_args]
        args += list(stage_args)
        n_seq, rows = steps, t_new * HEADS
        if stage == "scores":
            stage_out = ([jax.ShapeDtypeStruct((n_seq, rows, c.shape[2]), BF16) for c in caches_t]
                         + [jax.ShapeDtypeStruct((n_seq, rows, LANES), F32)] * n_roll)
        else:
            stage_out = [jax.ShapeDtypeStruct((n_seq, t_new, GROUP_WIDTH), F32)]
        out_specs += [per_seq(a) for a in stage_out]
        out_shape += stage_out
    outs = pl.pallas_call(
        functools.partial(_ffn_kernel, n_roll=n_roll, aliased=bool(aliases), t_new=t_new,
                          stage=stage),
        grid=(steps,),
        in_specs=in_specs,
        out_specs=out_specs,
        out_shape=out_shape,
        input_output_aliases=aliases,
        scratch_shapes=[pltpu.VMEM((tm, D_MODEL), F32)],
        compiler_params=_params(1, vmem=VMEM_LIMIT if roll is None else VMEM_LIMIT_FULL),
        name="ffn",
    )(*args)
    return outs[0] if roll is None else (outs[0], outs[1:1 + n_roll], outs[1 + n_roll:])


def _head_rmsnorm(x, g):
    low = lax.broadcasted_iota(jnp.int32, (1, LANES), 1) < HEAD_DIM
    outs = []
    for p in range(GROUP_WIDTH // LANES):
        xp = x[:, p * LANES:(p + 1) * LANES]
        sq = xp * xp
        s_all = jnp.sum(sq, axis=-1, keepdims=True)
        s_low = jnp.sum(jnp.where(low, sq, 0.0), axis=-1, keepdims=True)
        ms = jnp.where(low, s_low, s_all - s_low) * (1.0 / HEAD_DIM)
        outs.append(xp * lax.rsqrt(ms + EPS))
    return jnp.concatenate(outs, axis=-1) * g


def _in_kernel(x_ref, g_ref, w_ref, qn_ref, kn_ref, *refs, tm, dils, kv_from, kv_rows, row_kv):
    u_ref, gate_ref = refs[0:2]
    qkv_refs = refs[2:5]
    kvt_refs = refs[5:8]
    kvrow_refs = refs[8:11] if row_kv else None
    stage_ref, keep_ref = refs[-2:]
    j = pl.program_id(1)
    h = _rmsnorm(x_ref[...], g_ref[...]).astype(BF16)
    gate_ref[...] = jax.nn.sigmoid(_mm(h, w_ref[:, GATE_OFF:IN_WIDTH])).astype(gate_ref.dtype)
    for g in range(N_GROUPS):
        c0, c1 = g * GROUP_WIDTH, (g + 1) * GROUP_WIDTH
        q = _head_rmsnorm(_mm(h, w_ref[:, Q_OFF + c0:Q_OFF + c1]), qn_ref[:, c0:c1])
        k = _head_rmsnorm(_mm(h, w_ref[:, K_OFF + c0:K_OFF + c1]), kn_ref[:, c0:c1])
        v = _mm(h, w_ref[:, V_OFF + c0:V_OFF + c1])
        d = dils[g]
        if d == 1:
            qkv_refs[g][:, 0:GROUP_WIDTH] = q.astype(BF16)
            qkv_refs[g][:, GROUP_WIDTH:2 * GROUP_WIDTH] = k.astype(BF16)
            qkv_refs[g][:, 2 * GROUP_WIDTH:3 * GROUP_WIDTH] = v.astype(BF16)
        else:
            for ci, a in enumerate((q, k, v)):
                for cl in range(GROUP_WIDTH // LANES):
                    stage_ref[ci * (GROUP_WIDTH // LANES) + cl] = a[:, cl * LANES:(cl + 1) * LANES]
            for r in range(d):
                for cl in range(3 * GROUP_WIDTH // LANES):
                    qkv_refs[g][0, r, :, cl * LANES:(cl + 1) * LANES] = (
                        stage_ref[cl, pl.ds(r, tm // d, stride=d), :].astype(BF16))
        if row_kv:
            kvrow_refs[g][:, 0:GROUP_WIDTH] = k
            kvrow_refs[g][:, GROUP_WIDTH:2 * GROUP_WIDTH] = v

        r0 = kv_rows[g]
        if kv_from[g] == 0:
            kvt_refs[g][0, 0:GROUP_WIDTH, :] = k[r0:, :].T
            kvt_refs[g][0, GROUP_WIDTH:2 * GROUP_WIDTH, :] = v[r0:, :].T
        else:
            keep_ref[g, 0, 0:tm - r0, :] = k[r0:, :]
            keep_ref[g, 1, 0:tm - r0, :] = v[r0:, :]
    u_ref[...] = _mm(h, w_ref[:, 0:POOL_WIDTH])
    for g in range(N_GROUPS):
        if kv_from[g] > 0:
            @pl.when(j >= kv_from[g])
            def _(g=g):
                rows = tm - kv_rows[g]
                kvt_refs[g][0, 0:GROUP_WIDTH, :] = keep_ref[g, 0, 0:rows, :].T
                kvt_refs[g][0, GROUP_WIDTH:2 * GROUP_WIDTH, :] = keep_ref[g, 1, 0:rows, :].T


def _inproj(x1, g, w_in, qn, kn, n_seq, seq, tm, keeps, dils, row_kv):
    t = n_seq * seq
    nj = seq // tm
    row = lambda width: pl.BlockSpec((tm, width), lambda b, j: (b * nj + j, 0))
    qkv_specs, qkv_shapes = [], []
    for d in dils:
        if d == 1:
            qkv_specs.append(row(3 * GROUP_WIDTH))
            qkv_shapes.append(jax.ShapeDtypeStruct((t, 3 * GROUP_WIDTH), BF16))
        else:
            qkv_specs.append(pl.BlockSpec((1, d, tm // d, 3 * GROUP_WIDTH), lambda b, j: (b, 0, j, 0)))
            qkv_shapes.append(jax.ShapeDtypeStruct((n_seq, d, seq // d, 3 * GROUP_WIDTH), BF16))
    kv_from, kv_rows, kvt_specs, kvt_shapes = [], [], [], []
    for keep in keeps:
        tw = min(tm, keep)
        nb = keep // tw
        kv_from.append(nj - nb)
        kv_rows.append(tm - tw)
        kvt_specs.append(pl.BlockSpec(
            (1, 2 * GROUP_WIDTH, tw), lambda b, j, nb=nb: (b, 0, jnp.maximum(j - (nj - nb), 0))))
        kvt_shapes.append(jax.ShapeDtypeStruct((n_seq, 2 * GROUP_WIDTH, keep), F32))
    keep_rows = max([tm - r for r, f in zip(kv_rows, kv_from) if f > 0], default=8)
    out_specs = [row(POOL_WIDTH), row(2 * D_MODEL)] + qkv_specs + kvt_specs
    out_shape = [jax.ShapeDtypeStruct((t, POOL_WIDTH), F32),
                 jax.ShapeDtypeStruct((t, 2 * D_MODEL), BF16)] + qkv_shapes + kvt_shapes
    if row_kv:
        out_specs += [row(2 * GROUP_WIDTH)] * N_GROUPS
        out_shape += [jax.ShapeDtypeStruct((t, 2 * GROUP_WIDTH), F32)] * N_GROUPS
    return pl.pallas_call(
        functools.partial(_in_kernel, tm=tm, dils=tuple(dils), kv_from=tuple(kv_from),
                          kv_rows=tuple(kv_rows), row_kv=row_kv),
        grid=(n_seq, nj),
        in_specs=[
            row(D_MODEL),
            _resident((1, D_MODEL)),
            _resident((D_MODEL, IN_WIDTH)),
            _resident((1, ATT_WIDTH)),
            _resident((1, ATT_WIDTH)),
        ],
        out_specs=out_specs,
        out_shape=out_shape,
        scratch_shapes=[pltpu.VMEM((3 * GROUP_WIDTH // LANES, tm, LANES), F32),
                        pltpu.VMEM((N_GROUPS, 2, keep_rows, GROUP_WIDTH), F32)],
        compiler_params=_params(2, vmem=VMEM_LIMIT_FULL),
        name="inproj",
    )(x1, g, w_in, qn, kn)


def _attn_kernel(qkv_ref, o_ref, l_ref, *, n_blocks):
    n_streams = qkv_ref.shape[1]
    i = lax.broadcasted_iota(jnp.int32, (ATT_BLK, 2 * ATT_BLK), 0)
    j = lax.broadcasted_iota(jnp.int32, (ATT_BLK, 2 * ATT_BLK), 1)
    band = (j >= i) & (j <= i + ATT_BLK)
    first = band[:, ATT_BLK:]
    low = lax.broadcasted_iota(jnp.int32, (ATT_BLK, LANES), 1) < HEAD_DIM
    low_keys = {n: lax.broadcasted_iota(jnp.int32, (n, LANES), 1) < HEAD_DIM
                for n in (ATT_BLK, 2 * ATT_BLK)}
    scale = jnp.asarray(HEAD_DIM ** -0.5, BF16)

    def run(blocks):
        n_pairs = GROUP_WIDTH // LANES
        scores = []
        for r, q0, k0, n_keys, mask in blocks:
            for p in range(n_pairs):
                c = p * LANES
                qp = qkv_ref[0, r, pl.ds(q0, ATT_BLK), c:c + LANES] * scale
                kp = qkv_ref[0, r, pl.ds(k0, n_keys), GROUP_WIDTH + c:GROUP_WIDTH + c + LANES]
                for own in (low, ~low):
                    s = _mm_nt(jnp.where(own, qp, jnp.zeros_like(qp)), kp)
                    scores.append(jnp.where(mask, s, -jnp.inf))
        probs, mxs = [], []
        for s in scores:
            mx = jnp.max(s, axis=-1, keepdims=True)
            probs.append(jnp.exp((s - mx).astype(BF16)))
            mxs.append(mx)
        for bi, (r, q0, k0, n_keys, _) in enumerate(blocks):
            for p in range(n_pairs):
                c = p * LANES
                h0 = 2 * (bi * n_pairs + p)
                vp = qkv_ref[0, r, pl.ds(k0, n_keys), 2 * GROUP_WIDTH + c:2 * GROUP_WIDTH + c + LANES]
                own_k = low_keys[n_keys]
                one = jnp.ones_like(vp)
                t_lo = _mm(probs[h0], jnp.where(own_k, vp, one))
                t_hi = _mm(probs[h0 + 1], jnp.where(own_k, one, vp))
                den = pltpu.roll(jnp.where(low, t_hi, t_lo), HEAD_DIM, axis=1)
                o = jnp.where(low, t_lo, t_hi) * (1.0 / den)
                o_ref[0, r, pl.ds(q0, ATT_BLK), c:c + LANES] = o.astype(o_ref.dtype)
                l_ref[0, r, pl.ds(q0, ATT_BLK), c:c + LANES] = (
                    jnp.where(low, mxs[h0], mxs[h0 + 1]) + jnp.log(den))

    def first_block(r):
        return (r, 0, 0, ATT_BLK, first)

    def band_block(r, n):
        q0, k0 = n * ATT_BLK, (n - 1) * ATT_BLK
        if not isinstance(n, int):
            q0, k0 = pl.multiple_of(q0, ATT_BLK), pl.multiple_of(k0, ATT_BLK)
        return (r, q0, k0, 2 * ATT_BLK, band)

    per = BLOCKS_PER_TRIP
    if n_blocks == 1:
        def body(it, carry):
            run([first_block(per * it + i) for i in range(per)])
            return carry
        lax.fori_loop(0, n_streams // per, body, 0)
    else:
        trips = n_blocks // per

        def body(it, carry):
            r = it // trips
            m = it % trips

            @pl.when(m == 0)
            def _():
                run([first_block(r)] + [band_block(r, i) for i in range(1, per)])

            @pl.when(m > 0)
            def _():
                run([band_block(r, per * m + i) for i in range(per)])
            return carry
        lax.fori_loop(0, n_streams * trips, body, 0)


def _attn_prompt(qkv, n_seq, seq, dil):
    ln = seq // dil
    out = jax.ShapeDtypeStruct((n_seq, dil, ln, GROUP_WIDTH), F32)
    ospec = pl.BlockSpec((1, dil, ln, GROUP_WIDTH), lambda b: (b, 0, 0, 0))
    return pl.pallas_call(
        functools.partial(_attn_kernel, n_blocks=ln // ATT_BLK),
        grid=(n_seq,),
        in_specs=[pl.BlockSpec((1, dil, ln, 3 * GROUP_WIDTH), lambda b: (b, 0, 0, 0))],
        out_specs=[ospec, ospec],
        out_shape=[jax.ShapeDtypeStruct(out.shape, BF16), out],
        compiler_params=_params(1),
        name=f"attn_d{dil}",
    )(qkv)


def _combine_groups(os_, ls):
    mx = jnp.maximum(jnp.maximum(ls[0], ls[1]), ls[2])
    es = [jnp.exp(l - mx) for l in ls]
    inv = 1.0 / (es[0] + es[1] + es[2])
    return (es[0] * inv) * os_[0] + (es[1] * inv) * os_[1] + (es[2] * inv) * os_[2]


def _merge_tail(x1, pooled, att, gate, pw_ref, ps_ref, wbp_ref, wba_ref, wo_ref):
    mixed = jnp.concatenate(
        [_mm(pooled[gi].astype(BF16), pw_ref[gi]) for gi in range(len(POOL_WINDOWS))], axis=-1)
    pool_y = (mixed * ps_ref[...]).astype(BF16)
    merged = (gate[:, 0:D_MODEL] * _mm(pool_y, wbp_ref[...])
              + gate[:, D_MODEL:2 * D_MODEL] * _mm(att.astype(BF16), wba_ref[...]))
    return x1 + _mm(merged.astype(BF16), wo_ref[...])


def _window_sums(ext_ref, lvl_ref, gi, w, tm):
    n = tm + POOL_PAD
    cols = slice(gi * POOL_GROUP_WIDTH, (gi + 1) * POOL_GROUP_WIDTH)
    read = lambda lo, hi: ext_ref[lo:hi, cols]
    span, level = 1, gi * POOL_LEVELS
    while 2 * span < w:
        lvl_ref[level, 2 * span:n, :] = read(2 * span, n) + read(span, n - span)
        read = lambda lo, hi, level=level: lvl_ref[level, lo:hi, :]
        span, level = 2 * span, level + 1
    return read(POOL_PAD, POOL_PAD + tm) + read(POOL_PAD - span, POOL_PAD - span + tm)


def _merge_prompt_kernel(x1_ref, u_ref, up_ref, o0_ref, o1_ref, o2_ref, l0_ref, l1_ref, l2_ref,
                         gate_ref, pw_ref, ps_ref, wbp_ref, wba_ref, wo_ref, y_ref,
                         ext_ref, lvl_ref, nat_ref, *, tm):
    j = pl.program_id(1)
    u = u_ref[...]
    ext_ref[0:POOL_PAD, :] = jnp.where(j > 0, up_ref[...], 0.0)
    ext_ref[POOL_PAD:POOL_PAD + tm, :] = u
    pos = j * tm + lax.broadcasted_iota(jnp.int32, (tm, 1), 0)
    pooled = []
    for gi, w in enumerate(POOL_WINDOWS):
        c0 = gi * POOL_GROUP_WIDTH
        cnt = jnp.minimum(pos + 1, w).astype(F32)
        pooled.append(_window_sums(ext_ref, lvl_ref, gi, w, tm) / cnt
                      - u[:, c0:c0 + POOL_GROUP_WIDTH])

    def natural(ref, slot):
        if len(ref.shape) == 2:
            return ref[...].astype(F32)
        d = ref.shape[1]
        n_tiles = GROUP_WIDTH // LANES
        for r in range(d):
            for cl in range(n_tiles):
                nat_ref[slot * n_tiles + cl, pl.ds(r, tm // d, stride=d), :] = (
                    ref[0, r, :, cl * LANES:(cl + 1) * LANES].astype(F32))
        return jnp.concatenate([nat_ref[slot * n_tiles + cl] for cl in range(n_tiles)], axis=-1)

    att = _combine_groups((natural(o0_ref, 0), natural(o1_ref, 1), natural(o2_ref, 2)),
                          (natural(l0_ref, 3), natural(l1_ref, 4), natural(l2_ref, 5)))
    y_ref[...] = _merge_tail(x1_ref[...], pooled, att, gate_ref[...],
                             pw_ref, ps_ref, wbp_ref, wba_ref, wo_ref)


def _merge_prompt(x1, u, os_, ls, gate, pw, ps, wbp, wba, wo, n_seq, seq, tm):
    t = n_seq * seq
    nj = seq // tm
    row = lambda width: pl.BlockSpec((tm, width), lambda b, j: (b * nj + j, 0))
    prev = pl.BlockSpec((POOL_PAD, POOL_WIDTH),
                        lambda b, j: (jnp.maximum((b * nj + j) * (tm // POOL_PAD) - 1, 0), 0))
    att_args, att_specs = [], []
    for a in list(os_) + list(ls):
        d = a.shape[1]
        if d == 1:
            att_args.append(a.reshape(t, GROUP_WIDTH))
            att_specs.append(row(GROUP_WIDTH))
        else:
            att_args.append(a)
            att_specs.append(pl.BlockSpec((1, d, tm // d, GROUP_WIDTH), lambda b, j: (b, 0, j, 0)))
    return pl.pallas_call(
        functools.partial(_merge_prompt_kernel, tm=tm),
        grid=(n_seq, nj),
        in_specs=[row(D_MODEL), row(POOL_WIDTH), prev] + att_specs + [row(2 * D_MODEL)]
        + [_resident(a.shape) for a in (pw, ps, wbp, wba, wo)],
        out_specs=row(D_MODEL),
        out_shape=jax.ShapeDtypeStruct((t, D_MODEL), F32),
        scratch_shapes=[pltpu.VMEM((tm + POOL_PAD, POOL_WIDTH), F32),
                        pltpu.VMEM((len(POOL_WINDOWS) * POOL_LEVELS, tm + POOL_PAD,
                                    POOL_GROUP_WIDTH), F32),
                        pltpu.VMEM((2 * N_GROUPS * GROUP_WIDTH // LANES, tm, LANES), F32)],
        compiler_params=_params(2),
        name="merge_prompt",
    )(x1, u, u, *att_args, gate, pw, ps, wbp, wba, wo)


def _merge_sample_kernel(x1_ref, ctx_ref, att_ref, gate_ref, pw_ref, ps_ref, wbp_ref, wba_ref,
                         wo_ref, y_ref, *, t_new):
    n_seq = ctx_ref.shape[0]
    for t in range(t_new):
        def ctx_row(r, c0):
            return ctx_ref[:, r * POOL_WIDTH + c0:r * POOL_WIDTH + c0 + POOL_GROUP_WIDTH]
        pooled = []
        for gi, w in enumerate(POOL_WINDOWS):
            c0 = gi * POOL_GROUP_WIDTH
            last = POOL_STATE + t
            acc = ctx_row(last, c0)
            for k in range(1, w):
                acc = acc + ctx_row(last - k, c0)
            pooled.append(acc / float(min(w, last + 1)) - ctx_row(last, c0))
        rows = pl.ds(t * n_seq, n_seq)
        y_ref[rows, :] = _merge_tail(x1_ref[rows, :], pooled, att_ref[rows, :], gate_ref[rows, :],
                                     pw_ref, ps_ref, wbp_ref, wba_ref, wo_ref)


def _merge_sample(x1, ctx, att, gate, pw, ps, wbp, wba, wo, t_new):
    args = (x1, ctx, att, gate, pw, ps, wbp, wba, wo)
    return pl.pallas_call(
        functools.partial(_merge_sample_kernel, t_new=t_new),
        grid=(1,),
        in_specs=[pl.BlockSpec(a.shape, lambda i, nd=a.ndim: (0,) * nd) for a in args],
        out_specs=pl.BlockSpec(x1.shape, lambda i: (0, 0)),
        out_shape=jax.ShapeDtypeStruct(x1.shape, F32),
        compiler_params=_params(1),
        name="merge_sample",
    )(*args)


def _cache_to_cols(c):
    n_seq, w = c.shape[0], c.shape[1]
    return c.transpose(0, 2, 3, 4, 1).reshape(n_seq, 2 * GROUP_WIDTH, w)


def _cols_to_cache(ct):
    n_seq, _, w = ct.shape
    return ct.reshape(n_seq, 2, HEADS, HEAD_DIM, w).transpose(0, 4, 1, 2, 3)[None]


def _tile_plan(n_p, seq, n_s, t_new):
    ffn = n_p * seq // n_s
    prompt = 512
    sample = 256
    assert ffn * n_s == n_p * seq and ffn % 8 == 0
    assert seq % prompt == 0 and prompt % (BF16_ROWS * max(d for _, d in ATT_CONFIGS)) == 0
    assert prompt % POOL_PAD == 0
    assert (n_s * t_new) % sample == 0 and LANES % t_new == 0
    return {"ffn": ffn, "prompt": prompt, "sample": sample}


def kernel(x_prompt, x_sample, cache_kv_w128, cache_kv_w512, cache_kv_w2048, state_pool,
           ffn1_norm, ffn1_w_gu, ffn1_w_down, mix_norm, w_in, q_norm, k_norm, pool_w,
           pool_scale, w_branch_pool, w_branch_att, w_out, ffn2_norm, ffn2_w_gu, ffn2_w_down):
    depth = ffn1_norm.shape[0]
    n_p, seq, _ = x_prompt.shape
    n_s, t_new, _ = x_sample.shape
    n_tok = n_s * t_new
    tiles = _tile_plan(n_p, seq, n_s, t_new)
    assert depth == 1
    assert all(c.shape[2] == w for c, (w, _) in
               zip((cache_kv_w128, cache_kv_w512, cache_kv_w2048), ATT_CONFIGS))
    caches_t = [_cache_to_cols(c[0]) for c in (cache_kv_w128, cache_kv_w512, cache_kv_w2048)]
    dils = tuple(d for _, d in ATT_CONFIGS)

    g1 = ffn1_norm[0][None, :]
    g2 = ffn2_norm[0][None, :]
    gm = mix_norm[0][None, :]
    wgu1, wd1 = ffn1_w_gu[0].astype(BF16), ffn1_w_down[0].astype(BF16)
    wgu2, wd2 = ffn2_w_gu[0].astype(BF16), ffn2_w_down[0].astype(BF16)
    win = w_in[0].astype(BF16)
    qn = q_norm[0].reshape(1, ATT_WIDTH)
    kn = k_norm[0].reshape(1, ATT_WIDTH)
    pw = pool_w[0].astype(BF16)
    ps = pool_scale[0][None, :]
    wbp, wba, wo = (w_branch_pool[0].astype(BF16), w_branch_att[0].astype(BF16), w_out[0].astype(BF16))

    xs = x_sample.reshape(n_tok, D_MODEL)
    x1s = _ffn(xs, g1, wgu1, wd1, tm=n_tok)
    us, gates, sq0, sq1, sq2, st0, st1, st2, sk0, sk1, sk2 = _inproj(
        x1s, gm, win, qn, kn, 1, n_tok, tiles["sample"], (n_tok,) * N_GROUPS, (1,) * N_GROUPS,
        row_kv=True)
    new_cols = [a[0] for a in (st0, st1, st2)]
    q_s = jnp.concatenate([a[:, 0:GROUP_WIDTH] for a in (sq0, sq1, sq2)], axis=-1).astype(F32)
    q_s = q_s.reshape(n_s, t_new, ATT_WIDTH)
    new_rows = [a.reshape(n_s, t_new, 2 * GROUP_WIDTH) for a in (sk0, sk1, sk2)]

    xp = x_prompt.reshape(n_p * seq, D_MODEL)
    tm_ffn = tiles["ffn"]
    x1, half_rolled, probs_stats = _ffn(
        xp, g1, wgu1, wd1, tm_ffn, roll=(caches_t, new_cols, 0, t_new, None),
        stage_args=[q_s] + new_rows)
    keeps = tuple(min(w, seq) for w, _ in ATT_CONFIGS)
    u, gate, qkv0, qkv1, qkv2, kvt0, kvt1, kvt2 = _inproj(
        x1, gm, win, qn, kn, n_p, seq, tiles["prompt"], keeps, dils, row_kv=False)
    os_, ls = [], []
    for qkv, dil in zip((qkv0, qkv1, qkv2), dils):
        o, l = _attn_prompt(qkv.reshape(n_p, dil, seq // dil, 3 * GROUP_WIDTH), n_p, seq, dil)
        os_.append(o)
        ls.append(l)
    x2 = _merge_prompt(x1, u, os_, ls, gate, pw, ps, wbp, wba, wo, n_p, seq, tiles["prompt"])
    y_p, new_caches_t, (att_s,) = _ffn(
        x2, g2, wgu2, wd2, tm_ffn, roll=(caches_t, new_cols, 1, t_new, half_rolled),
        stage_args=list(probs_stats) + new_rows)
    y_prompt = y_p.reshape(n_p, seq, D_MODEL)
    new_kv_p = [_cols_to_cache(kvt) for kvt in (kvt0, kvt1, kvt2)]
    new_pool_p = u.reshape(n_p, seq, POOL_WIDTH)[None, :, seq - POOL_STATE:, :]
    new_kv_s = [_cols_to_cache(c) for c in new_caches_t]

    u_ctx = jnp.concatenate([state_pool[0], us.reshape(n_s, t_new, POOL_WIDTH)], axis=1)
    new_pool_s = u_ctx[None, :, t_new:, :]
    to_ts = lambda a: a.reshape(n_s, t_new, -1).transpose(1, 0, 2).reshape(n_tok, -1)
    x2s = _merge_sample(to_ts(x1s), u_ctx.reshape(n_s, (POOL_STATE + t_new) * POOL_WIDTH),
                        to_ts(att_s), to_ts(gates), pw, ps, wbp, wba, wo, t_new)
    y_s = _ffn(x2s, g2, wgu2, wd2, tm=n_tok)
    y_sample = y_s.reshape(t_new, n_s, D_MODEL).transpose(1, 0, 2)

    return (y_prompt, y_sample, new_kv_p[0], new_kv_p[1], new_kv_p[2], new_pool_p,
            new_kv_s[0], new_kv_s[1], new_kv_s[2], new_pool_s)
```

```python
import functools

import jax
import jax.numpy as jnp
from jax import lax
from jax.experimental import pallas as pl
from jax.experimental.pallas import tpu as pltpu

F32 = jnp.float32
BF16 = jnp.bfloat16

D_MODEL = 1024
D_FF = 2816
POOL_WINDOWS = (2, 4, 8, 16)
POOL_GROUP_WIDTH = 128
POOL_WIDTH = 512
POOL_STATE = 15
POOL_PAD = 16
POOL_LEVELS = 3
ATT_CONFIGS = ((128, 1), (512, 4), (2048, 16))
N_GROUPS = 3
HEADS = 8
HEAD_DIM = 64
GROUP_WIDTH = HEADS * HEAD_DIM
ATT_WIDTH = N_GROUPS * GROUP_WIDTH
ATT_BLK = 128
EPS = 1e-6
Q_OFF = POOL_WIDTH
K_OFF = Q_OFF + ATT_WIDTH
V_OFF = K_OFF + ATT_WIDTH
GATE_OFF = V_OFF + ATT_WIDTH
IN_WIDTH = GATE_OFF + 2 * D_MODEL

BLOCKS_PER_TRIP = 8
LANES = 128
BF16_ROWS = 16
FF_CHUNK = 256
VMEM_LIMIT = 56 * 1024 * 1024
VMEM_LIMIT_FULL = 62 * 1024 * 1024


def _params(n_axes, vmem=VMEM_LIMIT):
    return pltpu.CompilerParams(
        dimension_semantics=("arbitrary",) * n_axes, vmem_limit_bytes=vmem)


def _resident(shape):
    nd = len(shape)
    return pl.BlockSpec(shape, lambda *_: (0,) * nd, pipeline_mode=pl.Buffered(1))


def _rmsnorm(x, g):
    ms = jnp.mean(x * x, axis=-1, keepdims=True)
    return x * lax.rsqrt(ms + EPS) * g


def _mm(a, b):
    return jnp.dot(a, b, preferred_element_type=F32)


def _mm_nt(a, b):
    return lax.dot_general(a, b, (((1,), (1,)), ((), ())), preferred_element_type=F32)


def _roll_cache_half(c_ref, kvt_ref, n_ref, seq, t_new):
    w = c_ref.shape[2]
    tail_lane = lax.broadcasted_iota(jnp.int32, (GROUP_WIDTH, LANES), 1) >= LANES - t_new
    shift = (LANES - t_new) - t_new * (seq % (LANES // t_new))
    new_cols = pltpu.roll(kvt_ref[...], shift, axis=1)
    rolled = pltpu.roll(c_ref[0], w - t_new, axis=1)
    if w > LANES:
        n_ref[0, :, 0:w - LANES] = rolled[:, 0:w - LANES]
    n_ref[0, :, w - LANES:w] = jnp.where(tail_lane, new_cols, rolled[:, w - LANES:w])


def _sample_rows(t_new):
    rows = t_new * HEADS
    r_t = lax.broadcasted_iota(jnp.int32, (rows, 1), 0) // HEADS
    r_h = lax.broadcasted_iota(jnp.int32, (rows, GROUP_WIDTH), 0) % HEADS
    diag = (lax.broadcasted_iota(jnp.int32, (rows, GROUP_WIDTH), 1) // HEAD_DIM) == r_h
    return rows, r_t, diag


def _sample_scores(q_ref, kvn_refs, k_refs, p_refs, st_refs, t_new):
    rows, r_t, diag = _sample_rows(t_new)
    lane = lax.broadcasted_iota(jnp.int32, (rows, LANES), 1)
    for g in range(N_GROUPS):
        dil = ATT_CONFIGS[g][1]
        w = k_refs[g].shape[2]
        kn = kvn_refs[g][0][:, 0:GROUP_WIDTH].astype(BF16).astype(F32)
        qg = q_ref[0, :, g * GROUP_WIDTH:(g + 1) * GROUP_WIDTH].astype(BF16).astype(F32)
        qf = jnp.concatenate(
            [jnp.broadcast_to(qg[t:t + 1, :], (HEADS, GROUP_WIDTH)) for t in range(t_new)], axis=0)
        qf = jnp.where(diag, qf, 0.0)
        wi = lax.broadcasted_iota(jnp.int32, (rows, w), 1)
        ok_c = (wi >= r_t) & (((wi - r_t) & (dil - 1)) == 0)
        s_c = _mm(qf.astype(BF16), k_refs[g][0].astype(BF16)) * (HEAD_DIM ** -0.5)
        s_c = jnp.where(ok_c, s_c, -jnp.inf)
        s_n = []
        for t2 in range(t_new):
            ok = (r_t >= t2) & (((r_t - t2) & (dil - 1)) == 0)
            s = jnp.sum(qf * kn[t2:t2 + 1, :], axis=-1, keepdims=True) * (HEAD_DIM ** -0.5)
            s_n.append(jnp.where(ok, s, -jnp.inf))
        mx = jnp.max(s_c, axis=-1, keepdims=True)
        for s in s_n:
            mx = jnp.maximum(mx, s)
        e_c = jnp.exp(s_c - mx)
        e_n = [jnp.exp(s - mx) for s in s_n]
        den = jnp.sum(e_c, axis=-1, keepdims=True)
        for e in e_n:
            den = den + e
        inv = 1.0 / den
        p_refs[g][0] = (e_c * inv).astype(BF16)
        st = jnp.where(lane == t_new, mx + jnp.log(den), 0.0)
        for t2 in range(t_new):
            st = jnp.where(lane == t2, e_n[t2] * inv, st)
        st_refs[g][0] = st


def _sample_values(p_refs, st_refs, kvn_refs, v_refs, att_ref, t_new):
    rows, _, diag = _sample_rows(t_new)
    os_, ls = [], []
    for g in range(N_GROUPS):
        st = st_refs[g][0]
        vn = kvn_refs[g][0][:, GROUP_WIDTH:2 * GROUP_WIDTH].astype(BF16).astype(F32)
        o = _mm_nt(p_refs[g][0], v_refs[g][0].astype(BF16))
        for t2 in range(t_new):
            o = o + st[:, t2:t2 + 1].astype(BF16).astype(F32) * vn[t2:t2 + 1, :]
        os_.append(o)
        ls.append(st[:, t_new:t_new + 1])
    full = jnp.where(diag, _combine_groups(os_, ls), 0.0)
    att_ref[0] = jnp.concatenate(
        [jnp.sum(full[t * HEADS:(t + 1) * HEADS, :], axis=0, keepdims=True) for t in range(t_new)],
        axis=0)


def _ffn_kernel(x_ref, g_ref, wgu_ref, wd_ref, *refs, n_roll, aliased, t_new, stage):
    it = iter(refs)
    take = lambda n: [next(it) for _ in range(n)]
    c_refs, kvt_refs = take(n_roll), take(n_roll)
    take(n_roll if aliased else 0)
    if stage == "scores":
        q_ref, kvn_refs = next(it), take(n_roll)
    elif stage == "values":
        p_refs, st_refs, kvn_refs = take(n_roll), take(n_roll), take(n_roll)
    o_ref = next(it)
    n_refs = take(n_roll)
    if stage == "scores":
        _sample_scores(q_ref, kvn_refs, c_refs, take(n_roll), take(n_roll), t_new)
    elif stage == "values":
        _sample_values(p_refs, st_refs, kvn_refs, c_refs, next(it), t_new)
    acc_ref = next(it)
    for g in range(n_roll):
        _roll_cache_half(c_refs[g], kvt_refs[g], n_refs[g], pl.program_id(0), t_new)
    x = x_ref[...]
    xn = _rmsnorm(x, g_ref[...]).astype(BF16)
    for c, lo in enumerate(range(0, D_FF, FF_CHUNK)):
        hi = min(lo + FF_CHUNK, D_FF)
        a = _mm(xn, wgu_ref[:, lo:hi])
        b = _mm(xn, wgu_ref[:, D_FF + lo:D_FF + hi])
        h = (a * jax.nn.sigmoid(a) * b).astype(BF16)
        d = _mm(h, wd_ref[lo:hi, :])
        if c == 0:
            acc_ref[...] = d
        else:
            acc_ref[...] += d
    o_ref[...] = x + 0.5 * acc_ref[...]


def _ffn(x, g, wgu, wd, tm, roll=None, stage_args=()):
    t = x.shape[0]
    steps = t // tm
    in_specs = [
        pl.BlockSpec((tm, D_MODEL), lambda i: (i, 0)),
        _resident((1, D_MODEL)),
        _resident((D_MODEL, 2 * D_FF)),
        _resident((D_FF, D_MODEL)),
    ]
    args = [x, g, wgu, wd]
    out_specs = [pl.BlockSpec((tm, D_MODEL), lambda i: (i, 0))]
    out_shape = [jax.ShapeDtypeStruct((t, D_MODEL), F32)]
    aliases = {}
    n_roll, t_new, stage = 0, 0, None
    if roll is not None:
        caches_t, kvts, half, t_new, partial = roll
        n_roll = len(caches_t)
        assert all(c.shape[0] == steps for c in caches_t)
        per_tile = LANES // t_new
        half_specs = [pl.BlockSpec((1, GROUP_WIDTH, c.shape[2]), lambda i: (i, half, 0))
                      for c in caches_t]
        in_specs += half_specs
        in_specs += [pl.BlockSpec((GROUP_WIDTH, LANES), lambda i: (half, i // per_tile))] * n_roll
        args += list(caches_t) + list(kvts)
        if partial is not None:
            aliases = {len(args) + k: 1 + k for k in range(n_roll)}
            in_specs += [pl.BlockSpec(memory_space=pl.ANY)] * n_roll
            args += list(partial)
        out_specs += half_specs
        out_shape += [jax.ShapeDtypeStruct(c.shape, F32) for c in caches_t]
        stage = ("scores", "values")[half]
        per_seq = lambda a: pl.BlockSpec((1,) + a.shape[1:], lambda i: (i, 0, 0))
        in_specs += [per_seq(a) for a in stage_args]
        args += list(stage_args)
        n_seq, rows = steps, t_new * HEADS
        if stage == "scores":
            stage_out = ([jax.ShapeDtypeStruct((n_seq, rows, c.shape[2]), BF16) for c in caches_t]
                         + [jax.ShapeDtypeStruct((n_seq, rows, LANES), F32)] * n_roll)
        else:
            stage_out = [jax.ShapeDtypeStruct((n_seq, t_new, GROUP_WIDTH), F32)]
        out_specs += [per_seq(a) for a in stage_out]
        out_shape += stage_out
    outs = pl.pallas_call(
        functools.partial(_ffn_kernel, n_roll=n_roll, aliased=bool(aliases), t_new=t_new,
                          stage=stage),
        grid=(steps,),
        in_specs=in_specs,
        out_specs=out_specs,
        out_shape=out_shape,
        input_output_aliases=aliases,
        scratch_shapes=[pltpu.VMEM((tm, D_MODEL), F32)],
        compiler_params=_params(1, vmem=VMEM_LIMIT if roll is None else VMEM_LIMIT_FULL),
        name="ffn",
    )(*args)
    return outs[0] if roll is None else (outs[0], outs[1:1 + n_roll], outs[1 + n_roll:])


def _head_rmsnorm(x, g):
    low = lax.broadcasted_iota(jnp.int32, (1, LANES), 1) < HEAD_DIM
    outs = []
    for p in range(GROUP_WIDTH // LANES):
        xp = x[:, p * LANES:(p + 1) * LANES]
        sq = xp * xp
        s_all = jnp.sum(sq, axis=-1, keepdims=True)
        s_low = jnp.sum(jnp.where(low, sq, 0.0), axis=-1, keepdims=True)
        ms = jnp.where(low, s_low, s_all - s_low) * (1.0 / HEAD_DIM)
        outs.append(xp * lax.rsqrt(ms + EPS))
    return jnp.concatenate(outs, axis=-1) * g


def _in_kernel(x_ref, g_ref, w_ref, qn_ref, kn_ref, *refs, tm, dils, kv_from, kv_rows, row_kv):
    u_ref, gate_ref = refs[0:2]
    qkv_refs = refs[2:5]
    kvt_refs = refs[5:8]
    kvrow_refs = refs[8:11] if row_kv else None
    stage_ref, keep_ref = refs[-2:]
    j = pl.program_id(1)
    h = _rmsnorm(x_ref[...], g_ref[...]).astype(BF16)
    gate_ref[...] = jax.nn.sigmoid(_mm(h, w_ref[:, GATE_OFF:IN_WIDTH])).astype(gate_ref.dtype)
    for g in range(N_GROUPS):
        c0, c1 = g * GROUP_WIDTH, (g + 1) * GROUP_WIDTH
        q = _head_rmsnorm(_mm(h, w_ref[:, Q_OFF + c0:Q_OFF + c1]), qn_ref[:, c0:c1])
        k = _head_rmsnorm(_mm(h, w_ref[:, K_OFF + c0:K_OFF + c1]), kn_ref[:, c0:c1])
        v = _mm(h, w_ref[:, V_OFF + c0:V_OFF + c1])
        d = dils[g]
        if d == 1:
            qkv_refs[g][:, 0:GROUP_WIDTH] = q.astype(BF16)
            qkv_refs[g][:, GROUP_WIDTH:2 * GROUP_WIDTH] = k.astype(BF16)
            qkv_refs[g][:, 2 * GROUP_WIDTH:3 * GROUP_WIDTH] = v.astype(BF16)
        else:
            for ci, a in enumerate((q, k, v)):
                for cl in range(GROUP_WIDTH // LANES):
                    stage_ref[ci * (GROUP_WIDTH // LANES) + cl] = a[:, cl * LANES:(cl + 1) * LANES]
            for r in range(d):
                for cl in range(3 * GROUP_WIDTH // LANES):
                    qkv_refs[g][0, r, :, cl * LANES:(cl + 1) * LANES] = (
                        stage_ref[cl, pl.ds(r, tm // d, stride=d), :].astype(BF16))
        if row_kv:
            kvrow_refs[g][:, 0:GROUP_WIDTH] = k
            kvrow_refs[g][:, GROUP_WIDTH:2 * GROUP_WIDTH] = v

        r0 = kv_rows[g]
        if kv_from[g] == 0:
            kvt_refs[g][0, 0:GROUP_WIDTH, :] = k[r0:, :].T
            kvt_refs[g][0, GROUP_WIDTH:2 * GROUP_WIDTH, :] = v[r0:, :].T
        else:
            keep_ref[g, 0, 0:tm - r0, :] = k[r0:, :]
            keep_ref[g, 1, 0:tm - r0, :] = v[r0:, :]
    u_ref[...] = _mm(h, w_ref[:, 0:POOL_WIDTH])
    for g in range(N_GROUPS):
        if kv_from[g] > 0:
            @pl.when(j >= kv_from[g])
            def _(g=g):
                rows = tm - kv_rows[g]
                kvt_refs[g][0, 0:GROUP_WIDTH, :] = keep_ref[g, 0, 0:rows, :].T
                kvt_refs[g][0, GROUP_WIDTH:2 * GROUP_WIDTH, :] = keep_ref[g, 1, 0:rows, :].T


def _inproj(x1, g, w_in, qn, kn, n_seq, seq, tm, keeps, dils, row_kv):
    t = n_seq * seq
    nj = seq // tm
    row = lambda width: pl.BlockSpec((tm, width), lambda b, j: (b * nj + j, 0))
    qkv_specs, qkv_shapes = [], []
    for d in dils:
        if d == 1:
            qkv_specs.append(row(3 * GROUP_WIDTH))
            qkv_shapes.append(jax.ShapeDtypeStruct((t, 3 * GROUP_WIDTH), BF16))
        else:
            qkv_specs.append(pl.BlockSpec((1, d, tm // d, 3 * GROUP_WIDTH), lambda b, j: (b, 0, j, 0)))
            qkv_shapes.append(jax.ShapeDtypeStruct((n_seq, d, seq // d, 3 * GROUP_WIDTH), BF16))
    kv_from, kv_rows, kvt_specs, kvt_shapes = [], [], [], []
    for keep in keeps:
        tw = min(tm, keep)
        nb = keep // tw
        kv_from.append(nj - nb)
        kv_rows.append(tm - tw)
        kvt_specs.append(pl.BlockSpec(
            (1, 2 * GROUP_WIDTH, tw), lambda b, j, nb=nb: (b, 0, jnp.maximum(j - (nj - nb), 0))))
        kvt_shapes.append(jax.ShapeDtypeStruct((n_seq, 2 * GROUP_WIDTH, keep), F32))
    keep_rows = max([tm - r for r, f in zip(kv_rows, kv_from) if f > 0], default=8)
    out_specs = [row(POOL_WIDTH), row(2 * D_MODEL)] + qkv_specs + kvt_specs
    out_shape = [jax.ShapeDtypeStruct((t, POOL_WIDTH), F32),
                 jax.ShapeDtypeStruct((t, 2 * D_MODEL), BF16)] + qkv_shapes + kvt_shapes
    if row_kv:
        out_specs += [row(2 * GROUP_WIDTH)] * N_GROUPS
        out_shape += [jax.ShapeDtypeStruct((t, 2 * GROUP_WIDTH), F32)] * N_GROUPS
    return pl.pallas_call(
        functools.partial(_in_kernel, tm=tm, dils=tuple(dils), kv_from=tuple(kv_from),
                          kv_rows=tuple(kv_rows), row_kv=row_kv),
        grid=(n_seq, nj),
        in_specs=[
            row(D_MODEL),
            _resident((1, D_MODEL)),
            _resident((D_MODEL, IN_WIDTH)),
            _resident((1, ATT_WIDTH)),
            _resident((1, ATT_WIDTH)),
        ],
        out_specs=out_specs,
        out_shape=out_shape,
        scratch_shapes=[pltpu.VMEM((3 * GROUP_WIDTH // LANES, tm, LANES), F32),
                        pltpu.VMEM((N_GROUPS, 2, keep_rows, GROUP_WIDTH), F32)],
        compiler_params=_params(2, vmem=VMEM_LIMIT_FULL),
        name="inproj",
    )(x1, g, w_in, qn, kn)


def _attn_kernel(qkv_ref, o_ref, l_ref, *, n_blocks):
    n_streams = qkv_ref.shape[1]
    i = lax.broadcasted_iota(jnp.int32, (ATT_BLK, 2 * ATT_BLK), 0)
    j = lax.broadcasted_iota(jnp.int32, (ATT_BLK, 2 * ATT_BLK), 1)
    band = (j >= i) & (j <= i + ATT_BLK)
    first = band[:, ATT_BLK:]
    low = lax.broadcasted_iota(jnp.int32, (ATT_BLK, LANES), 1) < HEAD_DIM
    low_keys = {n: lax.broadcasted_iota(jnp.int32, (n, LANES), 1) < HEAD_DIM
                for n in (ATT_BLK, 2 * ATT_BLK)}
    scale = jnp.asarray(HEAD_DIM ** -0.5, BF16)

    def run(blocks):
        n_pairs = GROUP_WIDTH // LANES
        scores = []
        for r, q0, k0, n_keys, mask in blocks:
            for p in range(n_pairs):
                c = p * LANES
                qp = qkv_ref[0, r, pl.ds(q0, ATT_BLK), c:c + LANES] * scale
                kp = qkv_ref[0, r, pl.ds(k0, n_keys), GROUP_WIDTH + c:GROUP_WIDTH + c + LANES]
                for own in (low, ~low):
                    s = _mm_nt(jnp.where(own, qp, jnp.zeros_like(qp)), kp)
                    scores.append(jnp.where(mask, s, -jnp.inf))
        probs, mxs = [], []
        for s in scores:
            mx = jnp.max(s, axis=-1, keepdims=True)
            probs.append(jnp.exp((s - mx).astype(BF16)))
            mxs.append(mx)
        for bi, (r, q0, k0, n_keys, _) in enumerate(blocks):
            for p in range(n_pairs):
                c = p * LANES
                h0 = 2 * (bi * n_pairs + p)
                vp = qkv_ref[0, r, pl.ds(k0, n_keys), 2 * GROUP_WIDTH + c:2 * GROUP_WIDTH + c + LANES]
                own_k = low_keys[n_keys]
                one = jnp.ones_like(vp)
                t_lo = _mm(probs[h0], jnp.where(own_k, vp, one))
                t_hi = _mm(probs[h0 + 1], jnp.where(own_k, one, vp))
                den = pltpu.roll(jnp.where(low, t_hi, t_lo), HEAD_DIM, axis=1)
                o = jnp.where(low, t_lo, t_hi) * (1.0 / den)
                o_ref[0, r, pl.ds(q0, ATT_BLK), c:c + LANES] = o.astype(o_ref.dtype)
                l_ref[0, r, pl.ds(q0, ATT_BLK), c:c + LANES] = (
                    jnp.where(low, mxs[h0], mxs[h0 + 1]) + jnp.log(den))

    def first_block(r):
        return (r, 0, 0, ATT_BLK, first)

    def band_block(r, n):
        q0, k0 = n * ATT_BLK, (n - 1) * ATT_BLK
        if not isinstance(n, int):
            q0, k0 = pl.multiple_of(q0, ATT_BLK), pl.multiple_of(k0, ATT_BLK)
        return (r, q0, k0, 2 * ATT_BLK, band)

    per = min(BLOCKS_PER_TRIP, n_blocks * n_streams if n_blocks == 1 else n_blocks)
    if n_blocks == 1:
        def body(it, carry):
            run([first_block(per * it + i) for i in range(per)])
            return carry
        lax.fori_loop(0, n_streams // per, body, 0)
    else:
        trips = n_blocks // per

        def body(it, carry):
            r = it // trips
            m = it % trips

            @pl.when(m == 0)
            def _():
                run([first_block(r)] + [band_block(r, i) for i in range(1, per)])

            @pl.when(m > 0)
            def _():
                run([band_block(r, per * m + i) for i in range(per)])
            return carry
        lax.fori_loop(0, n_streams * trips, body, 0)


def _attn_prompt(qkv, n_seq, seq, dil):
    ln = seq // dil
    out = jax.ShapeDtypeStruct((n_seq, dil, ln, GROUP_WIDTH), F32)
    ospec = pl.BlockSpec((1, dil, ln, GROUP_WIDTH), lambda b: (b, 0, 0, 0))
    return pl.pallas_call(
        functools.partial(_attn_kernel, n_blocks=ln // ATT_BLK),
        grid=(n_seq,),
        in_specs=[pl.BlockSpec((1, dil, ln, 3 * GROUP_WIDTH), lambda b: (b, 0, 0, 0))],
        out_specs=[ospec, ospec],
        out_shape=[jax.ShapeDtypeStruct(out.shape, BF16), out],
        compiler_params=_params(1),
        name=f"attn_d{dil}",
    )(qkv)


def _combine_groups(os_, ls):
    mx = jnp.maximum(jnp.maximum(ls[0], ls[1]), ls[2])
    es = [jnp.exp(l - mx) for l in ls]
    inv = 1.0 / (es[0] + es[1] + es[2])
    return (es[0] * inv) * os_[0] + (es[1] * inv) * os_[1] + (es[2] * inv) * os_[2]


def _combine_groups_packed(os_, ls):
    mx = jnp.maximum(jnp.maximum(ls[0], ls[1]), ls[2])
    es = [jnp.exp((l - mx).astype(BF16)) for l in ls]
    inv = 1.0 / (es[0] + es[1] + es[2])
    os_ = [o.astype(BF16) for o in os_]
    return (es[0] * os_[0] + es[1] * os_[1] + es[2] * os_[2]) * inv


def _merge_tail(x1, pooled, att, gate, pw_ref, ps_ref, wbp_ref, wba_ref, wo_ref):
    mixed = jnp.concatenate(
        [_mm(pooled[gi].astype(BF16), pw_ref[gi]) for gi in range(len(POOL_WINDOWS))], axis=-1)
    pool_y = (mixed * ps_ref[...]).astype(BF16)
    merged = (gate[:, 0:D_MODEL] * _mm(pool_y, wbp_ref[...])
              + gate[:, D_MODEL:2 * D_MODEL] * _mm(att.astype(BF16), wba_ref[...]))
    return x1 + _mm(merged.astype(BF16), wo_ref[...])


def _window_sums(ext_ref, lvl_ref, gi, w, tm):
    n = tm + POOL_PAD
    cols = slice(gi * POOL_GROUP_WIDTH, (gi + 1) * POOL_GROUP_WIDTH)
    read = lambda lo, hi: ext_ref[lo:hi, cols]
    span, level = 1, gi * POOL_LEVELS
    while 2 * span < w:
        lvl_ref[level, 2 * span:n, :] = read(2 * span, n) + read(span, n - span)
        read = lambda lo, hi, level=level: lvl_ref[level, lo:hi, :]
        span, level = 2 * span, level + 1
    return read(POOL_PAD, POOL_PAD + tm) + read(POOL_PAD - span, POOL_PAD - span + tm)


def _merge_prompt_kernel(x1_ref, u_ref, up_ref, o0_ref, o1_ref, o2_ref, l0_ref, l1_ref, l2_ref,
                         gate_ref, pw_ref, ps_ref, wbp_ref, wba_ref, wo_ref, y_ref,
                         ext_ref, lvl_ref, nat_ref, *, tm):
    j = pl.program_id(1)
    u = u_ref[...]
    ext_ref[0:POOL_PAD, :] = jnp.where(j > 0, up_ref[...], 0.0)
    ext_ref[POOL_PAD:POOL_PAD + tm, :] = u
    pos = j * tm + lax.broadcasted_iota(jnp.int32, (tm, 1), 0)
    pooled = []
    for gi, w in enumerate(POOL_WINDOWS):
        c0 = gi * POOL_GROUP_WIDTH
        cnt = jnp.minimum(pos + 1, w).astype(F32)
        pooled.append(_window_sums(ext_ref, lvl_ref, gi, w, tm) / cnt
                      - u[:, c0:c0 + POOL_GROUP_WIDTH])

    def natural(ref, slot):
        if len(ref.shape) == 2:
            return ref[...].astype(F32)
        d = ref.shape[1]
        n_tiles = GROUP_WIDTH // LANES
        for r in range(d):
            for cl in range(n_tiles):
                nat_ref[slot * n_tiles + cl, pl.ds(r, tm // d, stride=d), :] = (
                    ref[0, r, :, cl * LANES:(cl + 1) * LANES].astype(F32))
        return jnp.concatenate([nat_ref[slot * n_tiles + cl] for cl in range(n_tiles)], axis=-1)

    att = _combine_groups_packed((natural(o0_ref, 0), natural(o1_ref, 1), natural(o2_ref, 2)),
                          (natural(l0_ref, 3), natural(l1_ref, 4), natural(l2_ref, 5)))
    y_ref[...] = _merge_tail(x1_ref[...], pooled, att, gate_ref[...],
                             pw_ref, ps_ref, wbp_ref, wba_ref, wo_ref)


def _merge_prompt(x1, u, os_, ls, gate, pw, ps, wbp, wba, wo, n_seq, seq, tm):
    t = n_seq * seq
    nj = seq // tm
    row = lambda width: pl.BlockSpec((tm, width), lambda b, j: (b * nj + j, 0))
    prev = pl.BlockSpec((POOL_PAD, POOL_WIDTH),
                        lambda b, j: (jnp.maximum((b * nj + j) * (tm // POOL_PAD) - 1, 0), 0))
    att_args, att_specs = [], []
    for a in list(os_) + list(ls):
        d = a.shape[1]
        if d == 1:
            att_args.append(a.reshape(t, GROUP_WIDTH))
            att_specs.append(row(GROUP_WIDTH))
        else:
            att_args.append(a)
            att_specs.append(pl.BlockSpec((1, d, tm // d, GROUP_WIDTH), lambda b, j: (b, 0, j, 0)))
    return pl.pallas_call(
        functools.partial(_merge_prompt_kernel, tm=tm),
        grid=(n_seq, nj),
        in_specs=[row(D_MODEL), row(POOL_WIDTH), prev] + att_specs + [row(2 * D_MODEL)]
        + [_resident(a.shape) for a in (pw, ps, wbp, wba, wo)],
        out_specs=row(D_MODEL),
        out_shape=jax.ShapeDtypeStruct((t, D_MODEL), F32),
        scratch_shapes=[pltpu.VMEM((tm + POOL_PAD, POOL_WIDTH), F32),
                        pltpu.VMEM((len(POOL_WINDOWS) * POOL_LEVELS, tm + POOL_PAD,
                                    POOL_GROUP_WIDTH), F32),
                        pltpu.VMEM((2 * N_GROUPS * GROUP_WIDTH // LANES, tm, LANES), F32)],
        compiler_params=_params(2),
        name="merge_prompt",
    )(x1, u, u, *att_args, gate, pw, ps, wbp, wba, wo)


def _merge_sample_kernel(x1_ref, ctx_ref, att_ref, gate_ref, pw_ref, ps_ref, wbp_ref, wba_ref,
                         wo_ref, y_ref, *, t_new):
    n_seq = ctx_ref.shape[0]
    for t in range(t_new):
        def ctx_row(r, c0):
            return ctx_ref[:, r * POOL_WIDTH + c0:r * POOL_WIDTH + c0 + POOL_GROUP_WIDTH]
        pooled = []
        for gi, w in enumerate(POOL_WINDOWS):
            c0 = gi * POOL_GROUP_WIDTH
            last = POOL_STATE + t
            acc = ctx_row(last, c0)
            for k in range(1, w):
                acc = acc + ctx_row(last - k, c0)
            pooled.append(acc / float(min(w, last + 1)) - ctx_row(last, c0))
        rows = pl.ds(t * n_seq, n_seq)
        y_ref[rows, :] = _merge_tail(x1_ref[rows, :], pooled, att_ref[rows, :], gate_ref[rows, :],
                                     pw_ref, ps_ref, wbp_ref, wba_ref, wo_ref)


def _merge_sample(x1, ctx, att, gate, pw, ps, wbp, wba, wo, t_new):
    args = (x1, ctx, att, gate, pw, ps, wbp, wba, wo)
    return pl.pallas_call(
        functools.partial(_merge_sample_kernel, t_new=t_new),
        grid=(1,),
        in_specs=[pl.BlockSpec(a.shape, lambda i, nd=a.ndim: (0,) * nd) for a in args],
        out_specs=pl.BlockSpec(x1.shape, lambda i: (0, 0)),
        out_shape=jax.ShapeDtypeStruct(x1.shape, F32),
        compiler_params=_params(1),
        name="merge_sample",
    )(*args)


def _cache_to_cols(c):
    n_seq, w = c.shape[0], c.shape[1]
    return c.transpose(0, 2, 3, 4, 1).reshape(n_seq, 2 * GROUP_WIDTH, w)


def _cols_to_cache(ct):
    n_seq, _, w = ct.shape
    return ct.reshape(n_seq, 2, HEADS, HEAD_DIM, w).transpose(0, 4, 1, 2, 3)[None]


def _tile_plan(n_p, seq, n_s, t_new):
    ffn = n_p * seq // n_s
    prompt = 512
    sample = 256
    assert ffn * n_s == n_p * seq and ffn % 8 == 0
    assert seq % prompt == 0 and prompt % (BF16_ROWS * max(d for _, d in ATT_CONFIGS)) == 0
    assert prompt % POOL_PAD == 0
    assert (n_s * t_new) % sample == 0 and LANES % t_new == 0
    return {"ffn": ffn, "prompt": prompt, "sample": sample}


def kernel(x_prompt, x_sample, cache_kv_w128, cache_kv_w512, cache_kv_w2048, state_pool,
           ffn1_norm, ffn1_w_gu, ffn1_w_down, mix_norm, w_in, q_norm, k_norm, pool_w,
           pool_scale, w_branch_pool, w_branch_att, w_out, ffn2_norm, ffn2_w_gu, ffn2_w_down):
    depth = ffn1_norm.shape[0]
    n_p, seq, _ = x_prompt.shape
    n_s, t_new, _ = x_sample.shape
    n_tok = n_s * t_new
    tiles = _tile_plan(n_p, seq, n_s, t_new)
    assert depth == 1
    assert all(c.shape[2] == w for c, (w, _) in
               zip((cache_kv_w128, cache_kv_w512, cache_kv_w2048), ATT_CONFIGS))
    caches_t = [_cache_to_cols(c[0]) for c in (cache_kv_w128, cache_kv_w512, cache_kv_w2048)]
    dils = tuple(d for _, d in ATT_CONFIGS)

    g1 = ffn1_norm[0][None, :]
    g2 = ffn2_norm[0][None, :]
    gm = mix_norm[0][None, :]
    wgu1, wd1 = ffn1_w_gu[0].astype(BF16), ffn1_w_down[0].astype(BF16)
    wgu2, wd2 = ffn2_w_gu[0].astype(BF16), ffn2_w_down[0].astype(BF16)
    win = w_in[0].astype(BF16)
    qn = q_norm[0].reshape(1, ATT_WIDTH)
    kn = k_norm[0].reshape(1, ATT_WIDTH)
    pw = pool_w[0].astype(BF16)
    ps = pool_scale[0][None, :]
    wbp, wba, wo = (w_branch_pool[0].astype(BF16), w_branch_att[0].astype(BF16), w_out[0].astype(BF16))

    xs = x_sample.reshape(n_tok, D_MODEL)
    x1s = _ffn(xs, g1, wgu1, wd1, tm=n_tok)
    us, gates, sq0, sq1, sq2, st0, st1, st2, sk0, sk1, sk2 = _inproj(
        x1s, gm, win, qn, kn, 1, n_tok, tiles["sample"], (n_tok,) * N_GROUPS, (1,) * N_GROUPS,
        row_kv=True)
    new_cols = [a[0] for a in (st0, st1, st2)]
    q_s = jnp.concatenate([a[:, 0:GROUP_WIDTH] for a in (sq0, sq1, sq2)], axis=-1).astype(F32)
    q_s = q_s.reshape(n_s, t_new, ATT_WIDTH)
    new_rows = [a.reshape(n_s, t_new, 2 * GROUP_WIDTH) for a in (sk0, sk1, sk2)]

    xp = x_prompt.reshape(n_p * seq, D_MODEL)
    tm_ffn = tiles["ffn"]
    x1, half_rolled, probs_stats = _ffn(
        xp, g1, wgu1, wd1, tm_ffn, roll=(caches_t, new_cols, 0, t_new, None),
        stage_args=[q_s] + new_rows)
    keeps = tuple(min(w, seq) for w, _ in ATT_CONFIGS)
    u, gate, qkv0, qkv1, qkv2, kvt0, kvt1, kvt2 = _inproj(
        x1, gm, win, qn, kn, n_p, seq, tiles["prompt"], keeps, dils, row_kv=False)
    os_, ls = [], []
    for qkv, dil in zip((qkv0, qkv1, qkv2), dils):
        o, l = _attn_prompt(qkv.reshape(n_p, dil, seq // dil, 3 * GROUP_WIDTH), n_p, seq, dil)
        os_.append(o)
        ls.append(l)
    x2 = _merge_prompt(x1, u, os_, ls, gate, pw, ps, wbp, wba, wo, n_p, seq, tiles["prompt"])
    y_p, new_caches_t, (att_s,) = _ffn(
        x2, g2, wgu2, wd2, tm_ffn, roll=(caches_t, new_cols, 1, t_new, half_rolled),
        stage_args=list(probs_stats) + new_rows)
    y_prompt = y_p.reshape(n_p, seq, D_MODEL)
    new_kv_p = [_cols_to_cache(kvt) for kvt in (kvt0, kvt1, kvt2)]
    new_pool_p = u.reshape(n_p, seq, POOL_WIDTH)[None, :, seq - POOL_STATE:, :]
    new_kv_s = [_cols_to_cache(c) for c in new_caches_t]

    u_ctx = jnp.concatenate([state_pool[0], us.reshape(n_s, t_new, POOL_WIDTH)], axis=1)
    new_pool_s = u_ctx[None, :, t_new:, :]
    to_ts = lambda a: a.reshape(n_s, t_new, -1).transpose(1, 0, 2).reshape(n_tok, -1)
    x2s = _merge_sample(to_ts(x1s), u_ctx.reshape(n_s, (POOL_STATE + t_new) * POOL_WIDTH),
                        to_ts(att_s), to_ts(gates), pw, ps, wbp, wba, wo, t_new)
    y_s = _ffn(x2s, g2, wgu2, wd2, tm=n_tok)
    y_sample = y_s.reshape(t_new, n_s, D_MODEL).transpose(1, 0, 2)

    return (y_prompt, y_sample, new_kv_p[0], new_kv_p[1], new_kv_p[2], new_pool_p,
            new_kv_s[0], new_kv_s[1], new_kv_s[2], new_pool_s)
```
